```python
import jax, jax.numpy as jnp
from jax import lax
import numpy as np

D_MODEL = 1024
BATCH = 16
SEQ = 2048
DEPTH = 2
DEC_BATCH = 8
DEC_SEQ = 32
PAST_LEN = 2048

CHUNK = 64
D_CONV = 1024
CONV_W = 3
N_HEADS = 16
QK_NOPE = 64
QK_ROPE = 32
V_HEAD = 64
Q_LORA = 384
KV_LORA = 256
D_FF = 2816
ROPE_THETA = 10000.0
EPS = 1e-6
Q_BLOCK = 128
ATTN_SCALE = (QK_NOPE + QK_ROPE) ** -0.5
NEG_INF = -1e30
D_IN = 3 * D_CONV + Q_LORA + KV_LORA + QK_ROPE + 2 * D_MODEL

kernel_name = "hybrid_shortconv_mla_stream_step"


def rmsnorm(x, g):
    xf = x.astype(jnp.float32)
    y = xf * lax.rsqrt(jnp.mean(xf * xf, axis=-1, keepdims=True) + EPS)
    return (y * g.astype(jnp.float32)).astype(x.dtype)


def apply_rope(x, pos):
    half = QK_ROPE // 2
    inv = ROPE_THETA ** (-jnp.arange(half, dtype=jnp.float32) / half)
    ang = pos.astype(jnp.float32)[:, None] * inv[None, :]
    cos = jnp.cos(ang)[None, :, None, :]
    sin = jnp.sin(ang)[None, :, None, :]
    xf = x.astype(jnp.float32)
    x1, x2 = xf[..., :half], xf[..., half:]
    return jnp.concatenate([x1 * cos - x2 * sin, x1 * sin + x2 * cos], axis=-1).astype(x.dtype)


def attend_block(q_nope, q_rope, q_pos, k_nope, k_rope, v, k_pos):
    s = (jnp.einsum('bqhn,bkhn->bhqk', q_nope, k_nope)
         + jnp.einsum('bqhr,bkr->bhqk', q_rope, k_rope))
    s = s.astype(jnp.float32) * ATTN_SCALE
    mask = (k_pos // CHUNK)[None, :] <= (q_pos // CHUNK)[:, None]
    s = jnp.where(mask[None, None], s, jnp.float32(NEG_INF))
    p = jax.nn.softmax(s, axis=-1).astype(v.dtype)
    return jnp.einsum('bhqk,bkhv->bqhv', p, v)


def mixer(h, pos, conv_hist, ckv_past, krope_past, k_pos, w_in, norm_q, norm_kv,
          w_uq, w_ukv, conv_w, w_conv_out, w_attn_out, w_merge):
    bsz, s_len, _ = h.shape
    proj = h @ w_in
    o1, o2, o3 = D_CONV, 2 * D_CONV, 3 * D_CONV
    o4 = o3 + Q_LORA
    o5 = o4 + KV_LORA
    o6 = o5 + QK_ROPE
    o7 = o6 + D_MODEL
    b_g, c_g, xin, q_lat, ckv, k_r, g_a, g_b = jnp.split(proj, [o1, o2, o3, o4, o5, o6, o7], axis=-1)

    u = c_g * xin
    u_ext = jnp.concatenate([conv_hist, u], axis=1)
    conv = (conv_w[0] * u_ext[:, 0:s_len] + conv_w[1] * u_ext[:, 1:s_len + 1]
            + conv_w[2] * u_ext[:, 2:s_len + 2])
    y_a = (b_g * conv) @ w_conv_out
    new_conv = u_ext[:, -(CONV_W - 1):]

    q = (rmsnorm(q_lat, norm_q) @ w_uq).reshape(bsz, s_len, N_HEADS, QK_NOPE + QK_ROPE)
    q_nope, q_rope = q[..., :QK_NOPE], apply_rope(q[..., QK_NOPE:], pos)
    c_new = rmsnorm(ckv, norm_kv)
    kr_new = apply_rope(k_r[:, :, None, :], pos)[:, :, 0]
    if ckv_past is None:
        c_all, kr_all = c_new, kr_new
    else:
        c_all = jnp.concatenate([ckv_past, c_new], axis=1)
        kr_all = jnp.concatenate([krope_past, kr_new], axis=1)
    n_keys = c_all.shape[1]
    kv = (c_all @ w_ukv).reshape(bsz, n_keys, N_HEADS, QK_NOPE + V_HEAD)
    k_nope, v = kv[..., :QK_NOPE], kv[..., QK_NOPE:]
    if ckv_past is None:
        n_blk = s_len // Q_BLOCK
        qn_b = jnp.moveaxis(q_nope.reshape(bsz, n_blk, Q_BLOCK, N_HEADS, QK_NOPE), 1, 0)
        qr_b = jnp.moveaxis(q_rope.reshape(bsz, n_blk, Q_BLOCK, N_HEADS, QK_ROPE), 1, 0)
        pos_b = pos.reshape(n_blk, Q_BLOCK)
        out = lax.map(lambda a: attend_block(a[0], a[1], a[2], k_nope, kr_all, v, k_pos),
                      (qn_b, qr_b, pos_b))
        attn = jnp.moveaxis(out, 0, 1).reshape(bsz, s_len, N_HEADS * V_HEAD)
    else:
        attn = attend_block(q_nope, q_rope, pos, k_nope, kr_all, v, k_pos).reshape(bsz, s_len, N_HEADS * V_HEAD)
    y_b = attn @ w_attn_out

    mixed = (jax.nn.sigmoid(g_a) * y_a + jax.nn.sigmoid(g_b) * y_b) @ w_merge
    return mixed, new_conv, c_new, kr_new


def layer(x, pos, k_pos, conv_hist, ckv_past, krope_past, w_in, norm_attn_pre, norm_attn_post,
          norm_q, norm_kv, w_uq, w_ukv, conv_w, w_conv_out, w_attn_out, w_merge,
          norm_ffn_pre, norm_ffn_post, w_gate_up, w_down):
    m, new_conv, c_new, kr_new = mixer(rmsnorm(x, norm_attn_pre), pos, conv_hist, ckv_past, krope_past,
                                       k_pos, w_in, norm_q, norm_kv, w_uq, w_ukv, conv_w,
                                       w_conv_out, w_attn_out, w_merge)
    x = x + rmsnorm(m, norm_attn_post)
    gu = rmsnorm(x, norm_ffn_pre) @ w_gate_up
    f = (jax.nn.silu(gu[..., :D_FF]) * gu[..., D_FF:]) @ w_down
    x = x + rmsnorm(f, norm_ffn_post)
    return x, new_conv, c_new, kr_new


def setup_inputs(seed: int = 0) -> dict:
    key = jax.random.key(seed)
    ks = jax.random.split(key, 24)
    f32 = jnp.float32

    def nrm(k, shape, scale):
        return jax.random.normal(k, shape, f32) * scale

    def gain(k, n):
        return 1.0 + 0.05 * jax.random.normal(k, (DEPTH, n), f32)

    return {
        "x_prompt": nrm(ks[0], (BATCH, SEQ, D_MODEL), 1.0),
        "x_sample": nrm(ks[1], (DEC_BATCH, DEC_SEQ, D_MODEL), 1.0),
        "state_conv": nrm(ks[2], (DEPTH, DEC_BATCH, CONV_W - 1, D_CONV), 1.0),
        "cache_ckv": nrm(ks[3], (DEPTH, DEC_BATCH, PAST_LEN, KV_LORA), 1.0),
        "cache_krope": nrm(ks[4], (DEPTH, DEC_BATCH, PAST_LEN, QK_ROPE), 1.0),
        "w_in": nrm(ks[5], (DEPTH, D_MODEL, D_IN), D_MODEL ** -0.5),
        "norm_attn_pre": gain(ks[6], D_MODEL),
        "norm_attn_post": gain(ks[7], D_MODEL),
        "norm_q": gain(ks[8], Q_LORA),
        "norm_kv": gain(ks[9], KV_LORA),
        "w_uq": nrm(ks[10], (DEPTH, Q_LORA, N_HEADS * (QK_NOPE + QK_ROPE)), Q_LORA ** -0.5),
        "w_ukv": nrm(ks[11], (DEPTH, KV_LORA, N_HEADS * (QK_NOPE + V_HEAD)), KV_LORA ** -0.5),
        "conv_w": nrm(ks[12], (DEPTH, CONV_W, D_CONV), CONV_W ** -0.5),
        "w_conv_out": nrm(ks[13], (DEPTH, D_CONV, D_MODEL), D_CONV ** -0.5),
        "w_attn_out": nrm(ks[14], (DEPTH, N_HEADS * V_HEAD, D_MODEL), (N_HEADS * V_HEAD) ** -0.5),
        "w_merge": nrm(ks[15], (DEPTH, D_MODEL, D_MODEL), D_MODEL ** -0.5),
        "norm_ffn_pre": gain(ks[16], D_MODEL),
        "norm_ffn_post": gain(ks[17], D_MODEL),
        "w_gate_up": nrm(ks[18], (DEPTH, D_MODEL, 2 * D_FF), D_MODEL ** -0.5),
        "w_down": nrm(ks[19], (DEPTH, D_FF, D_MODEL), D_FF ** -0.5),
    }


def reference(x_prompt, x_sample, state_conv, cache_ckv, cache_krope, w_in, norm_attn_pre,
              norm_attn_post, norm_q, norm_kv, w_uq, w_ukv, conv_w, w_conv_out, w_attn_out,
              w_merge, norm_ffn_pre, norm_ffn_post, w_gate_up, w_down):
    s_p = x_prompt.shape[1]
    s_s = x_sample.shape[1]
    past = cache_ckv.shape[2]
    pos_p = jnp.arange(s_p, dtype=jnp.int32)
    pos_s = past + jnp.arange(s_s, dtype=jnp.int32)
    kpos_s = jnp.arange(past + s_s, dtype=jnp.int32)
    hist_p = jnp.zeros((x_prompt.shape[0], CONV_W - 1, D_CONV), x_prompt.dtype)

    xp, xs = x_prompt, x_sample
    conv_p, ckv_p, kr_p, conv_s, ckv_s, kr_s = [], [], [], [], [], []
    for l in range(DEPTH):
        w = (w_in[l], norm_attn_pre[l], norm_attn_post[l], norm_q[l], norm_kv[l], w_uq[l], w_ukv[l],
             conv_w[l], w_conv_out[l], w_attn_out[l], w_merge[l], norm_ffn_pre[l], norm_ffn_post[l],
             w_gate_up[l], w_down[l])
        xp, nc, cn, kn = layer(xp, pos_p, pos_p, hist_p, None, None, *w)
        conv_p.append(nc); ckv_p.append(cn); kr_p.append(kn)
        xs, nc, cn, kn = layer(xs, pos_s, kpos_s, state_conv[l], cache_ckv[l], cache_krope[l], *w)
        conv_s.append(nc); ckv_s.append(cn); kr_s.append(kn)

    return (xp, xs, jnp.stack(conv_p), jnp.stack(ckv_p), jnp.stack(kr_p),
            jnp.stack(conv_s), jnp.stack(ckv_s), jnp.stack(kr_s))
```

```python
import functools

import jax
import jax.numpy as jnp
import numpy as np
from jax import lax
from jax.experimental import pallas as pl
from jax.experimental.pallas import tpu as pltpu

D_MODEL = 1024
CHUNK = 64
D_CONV = 1024
CONV_W = 3
N_HEADS = 16
QK_NOPE = 64
QK_ROPE = 32
V_HEAD = 64
Q_LORA = 384
KV_LORA = 256
D_FF = 2816
ROPE_THETA = 10000.0
EPS = 1e-6
ATTN_SCALE = (QK_NOPE + QK_ROPE) ** -0.5
NEG_INF = -1e30

LANES = 128
SLOT = LANES
N_PAIRS = N_HEADS // 2
O_B, O_C, O_X = 0, D_CONV, 2 * D_CONV
O_QL = 3 * D_CONV
O_KV = O_QL + Q_LORA
O_KR = O_KV + KV_LORA
O_KRR = O_KR + SLOT
O_GA = O_KRR + SLOT
O_GB = O_GA + D_MODEL
D_INP = O_GB + D_MODEL
VMEM_LIMIT = 56 * 1024 * 1024

F32 = jnp.float32
BF16 = jnp.bfloat16


def _rms(x, g):
    return x * lax.rsqrt(jnp.mean(x * x, axis=-1, keepdims=True) + EPS) * g


def _sigmoid(x):
    return 1.0 / (1.0 + jnp.exp(-x))


def _dot(a, b):
    return jnp.dot(a, b, preferred_element_type=F32)


def _const_spec(shape):
    nd = len(shape)
    return pl.BlockSpec(shape, lambda *_: (0,) * nd, pipeline_mode=pl.Buffered(1))


def _proj_kernel(x_ref, hist_ref, cos_ref, sin_ref, gpre_ref, gq_ref, gkv_ref, cw_ref,
                 win_ref, wuq_ref, wukv_ref, wco_ref,
                 q_ref, k_ref, v_ref, ta_ref, sgb_ref, c_ref, kr_ref, nc_ref,
                 ubuf, *, tm, tiles_per_seq):
    i = pl.program_id(0)
    first = (i % tiles_per_seq) == 0

    @pl.when(first)
    def _():
        ubuf[6:8, :] = hist_ref[...]

    @pl.when(jnp.logical_not(first))
    def _():
        ubuf[0:8, :] = ubuf[tm:tm + 8, :]

    x = x_ref[...]
    h = _rms(x, gpre_ref[...]).astype(BF16)

    cx = _dot(h, win_ref[:, O_C:O_C + 2 * D_CONV])
    u = cx[:, :D_CONV] * cx[:, D_CONV:]
    ubuf[8:8 + tm, :] = u
    cw = cw_ref[...]
    conv = (cw[0:1, :] * ubuf[6:6 + tm, :] + cw[1:2, :] * ubuf[7:7 + tm, :] + cw[2:3, :] * u)
    bg = _dot(h, win_ref[:, O_B:O_B + D_CONV])
    ya = _dot((bg * conv).astype(BF16), wco_ref[...])
    ga = _dot(h, win_ref[:, O_GA:O_GA + D_MODEL])
    ta_ref[...] = (_sigmoid(ga) * ya).astype(BF16)
    gb = _dot(h, win_ref[:, O_GB:O_GB + D_MODEL])
    sgb_ref[...] = _sigmoid(gb).astype(BF16)
    nc_ref[...] = ubuf[tm + 6:tm + 8, :]

    cos = cos_ref[...]
    sin = sin_ref[...]
    small = _dot(h, win_ref[:, O_QL:O_GA])
    qn = _rms(small[:, :Q_LORA], gq_ref[...]).astype(BF16)
    c_new = _rms(small[:, Q_LORA:Q_LORA + KV_LORA], gkv_ref[...])
    c_ref[...] = c_new
    kr_slot = (small[:, O_KR - O_QL:O_KR - O_QL + SLOT] * cos
               + small[:, O_KRR - O_QL:O_KRR - O_QL + SLOT] * sin)
    kr_ref[...] = kr_slot[:, :QK_ROPE]

    qq = _dot(qn, wuq_ref[...])
    kvx = _dot(c_new.astype(BF16), wukv_ref[...])
    qw = N_HEADS * SLOT
    for hd in range(N_HEADS):
        a = qq[:, hd * SLOT:(hd + 1) * SLOT]
        b = qq[:, qw + hd * SLOT:qw + (hd + 1) * SLOT]
        q_ref[:, hd * SLOT:(hd + 1) * SLOT] = ((a * cos + b * sin) * ATTN_SCALE).astype(BF16)
        k_ref[:, hd * SLOT:(hd + 1) * SLOT] = (kvx[:, hd * SLOT:(hd + 1) * SLOT] + kr_slot).astype(BF16)
    v_ref[...] = kvx[:, qw:].astype(BF16)


def _proj(x2d, hist, cos_t, sin_t, pos_tile0, lw, *, n_seq, seq_len, tm):
    n_tok = n_seq * seq_len
    tiles_per_seq = seq_len // tm
    assert seq_len % tm == 0 and seq_len >= CONV_W - 1
    grid = (n_tok // tm,)
    tok = lambda w: pl.BlockSpec((tm, w), lambda i: (i, 0))
    pos = pl.BlockSpec((tm, SLOT), lambda i: (pos_tile0 + i % tiles_per_seq, 0))
    seq3 = pl.BlockSpec((None, CONV_W - 1, D_CONV), lambda i: (i // tiles_per_seq, 0, 0))
    out_shape = (
        jax.ShapeDtypeStruct((n_tok, N_HEADS * SLOT), BF16),
        jax.ShapeDtypeStruct((n_tok, N_HEADS * SLOT), BF16),
        jax.ShapeDtypeStruct((n_tok, N_HEADS * V_HEAD), BF16),
        jax.ShapeDtypeStruct((n_tok, D_MODEL), BF16),
        jax.ShapeDtypeStruct((n_tok, D_MODEL), BF16),
        jax.ShapeDtypeStruct((n_tok, KV_LORA), F32),
        jax.ShapeDtypeStruct((n_tok, QK_ROPE), F32),
        jax.ShapeDtypeStruct((n_seq, CONV_W - 1, D_CONV), F32),
    )
    return pl.pallas_call(
        functools.partial(_proj_kernel, tm=tm, tiles_per_seq=tiles_per_seq),
        grid=grid,
        in_specs=[tok(D_MODEL), seq3, pos, pos,
                  _const_spec((1, D_MODEL)), _const_spec((1, Q_LORA)), _const_spec((1, KV_LORA)),
                  _const_spec((CONV_W, D_CONV)),
                  _const_spec(lw["w_in"].shape), _const_spec(lw["w_uq"].shape),
                  _const_spec(lw["w_ukv"].shape), _const_spec(lw["w_conv_out"].shape)],
        out_specs=(tok(N_HEADS * SLOT), tok(N_HEADS * SLOT), tok(N_HEADS * V_HEAD),
                   tok(D_MODEL), tok(D_MODEL), tok(KV_LORA), tok(QK_ROPE), seq3),
        out_shape=out_shape,
        scratch_shapes=[pltpu.VMEM((tm + 8, D_CONV), F32)],
        compiler_params=pltpu.CompilerParams(dimension_semantics=("arbitrary",),
                                             vmem_limit_bytes=VMEM_LIMIT),
        name="proj",
    )(x2d, hist, cos_t, sin_t, lw["g_pre"], lw["g_q"], lw["g_kv"], lw["conv_w"],
      lw["w_in"], lw["w_uq"], lw["w_ukv"], lw["w_conv_out"])


def _flash_step(q, k, v, carry, mask):
    m, l, acc = carry
    s = lax.dot_general(q, k, (((1,), (1,)), ((), ())), preferred_element_type=F32)
    if mask is not None:
        s = jnp.where(mask, s, NEG_INF)
    m_new = jnp.maximum(m, jnp.max(s, axis=-1, keepdims=True))
    alpha = jnp.exp(m - m_new)
    p = jnp.exp(s - m_new)
    l = alpha * l + jnp.sum(p, axis=-1, keepdims=True)
    acc = alpha * acc + _dot(p.astype(BF16), v)
    return m_new, l, acc


def _attn_kernel(q_ref, k_ref, v_ref, o_ref, *, seq_len, tq):
    n_q = seq_len // tq
    row = lax.broadcasted_iota(jnp.int32, (tq, tq), 0)
    col = lax.broadcasted_iota(jnp.int32, (tq, tq), 1)
    diag_mask = (col // CHUNK) <= (row // CHUNK)
    lane = lax.broadcasted_iota(jnp.int32, (tq, 2 * V_HEAD), 1)
    for qi in range(n_q):
        outs = []
        for hh in range(2):
            q = q_ref[qi * tq:(qi + 1) * tq, hh * SLOT:(hh + 1) * SLOT]

            def body(j, carry, q=q, hh=hh):
                r0 = pl.multiple_of(j * tq, tq)
                k = k_ref[pl.ds(r0, tq), hh * SLOT:(hh + 1) * SLOT]
                v = v_ref[pl.ds(r0, tq), :]
                return _flash_step(q, k, v, carry, None)

            carry = (jnp.full((tq, 1), NEG_INF, F32), jnp.zeros((tq, 1), F32),
                     jnp.zeros((tq, 2 * V_HEAD), F32))
            if qi > 0:
                carry = lax.fori_loop(0, qi, body, carry)
            k = k_ref[qi * tq:(qi + 1) * tq, hh * SLOT:(hh + 1) * SLOT]
            v = v_ref[qi * tq:(qi + 1) * tq, :]
            m, l, acc = _flash_step(q, k, v, carry, diag_mask)
            outs.append(acc / l)
        o_ref[qi * tq:(qi + 1) * tq, :] = jnp.where(lane < V_HEAD, outs[0], outs[1]).astype(BF16)


def _attn(q3, k3, v3, *, tq):
    n_seq, seq_len, _ = q3.shape
    assert seq_len % tq == 0 and tq % CHUNK == 0
    return pl.pallas_call(
        functools.partial(_attn_kernel, seq_len=seq_len, tq=tq),
        grid=(n_seq, N_PAIRS),
        in_specs=[pl.BlockSpec((None, seq_len, 2 * SLOT), lambda b, p: (b, 0, p)),
                  pl.BlockSpec((None, seq_len, 2 * SLOT), lambda b, p: (b, 0, p)),
                  pl.BlockSpec((None, seq_len, 2 * V_HEAD), lambda b, p: (b, 0, p))],
        out_specs=pl.BlockSpec((None, seq_len, 2 * V_HEAD), lambda b, p: (b, 0, p)),
        out_shape=jax.ShapeDtypeStruct((n_seq, seq_len, N_HEADS * V_HEAD), BF16),
        compiler_params=pltpu.CompilerParams(dimension_semantics=("arbitrary", "arbitrary"),
                                             vmem_limit_bytes=VMEM_LIMIT),
        name="attn",
    )(q3, k3, v3)


def _kv_expand_kernel(c_ref, kr_ref, wukv_ref, place_ref, k_ref, v_ref):
    kvx = _dot(c_ref[...].astype(BF16), wukv_ref[...])
    kr_slot = _dot(kr_ref[...].astype(BF16), place_ref[...])
    qw = N_HEADS * SLOT
    for hd in range(N_HEADS):
        k_ref[:, hd * SLOT:(hd + 1) * SLOT] = (kvx[:, hd * SLOT:(hd + 1) * SLOT] + kr_slot).astype(BF16)
    v_ref[...] = kvx[:, qw:].astype(BF16)


def _kv_expand(c2d, kr2d, lw, *, tm):
    n_tok = c2d.shape[0]
    tok = lambda w: pl.BlockSpec((tm, w), lambda i: (i, 0))
    return pl.pallas_call(
        _kv_expand_kernel,
        grid=(n_tok // tm,),
        in_specs=[tok(KV_LORA), tok(QK_ROPE), _const_spec(lw["w_ukv"].shape),
                  _const_spec(lw["place"].shape)],
        out_specs=(tok(N_HEADS * SLOT), tok(N_HEADS * V_HEAD)),
        out_shape=(jax.ShapeDtypeStruct((n_tok, N_HEADS * SLOT), BF16),
                   jax.ShapeDtypeStruct((n_tok, N_HEADS * V_HEAD), BF16)),
        compiler_params=pltpu.CompilerParams(dimension_semantics=("arbitrary",),
                                             vmem_limit_bytes=VMEM_LIMIT),
        name="kv_expand",
    )(c2d, kr2d, lw["w_ukv"], lw["place"])


def _attn_sample_kernel(q_ref, kp_ref, kn_ref, vp_ref, vn_ref, o_ref, *, mask_new):
    s_len = q_ref.shape[0]
    lane = lax.broadcasted_iota(jnp.int32, (s_len, 2 * V_HEAD), 1)
    outs = []
    for hh in range(2):
        sl = slice(hh * SLOT, (hh + 1) * SLOT)
        q = q_ref[:, sl]
        dn = (((1,), (1,)), ((), ()))
        s_p = lax.dot_general(q, kp_ref[:, sl], dn, preferred_element_type=F32)
        s_n = lax.dot_general(q, kn_ref[:, sl], dn, preferred_element_type=F32)
        if mask_new is not None:
            s_n = jnp.where(mask_new, s_n, NEG_INF)
        m = jnp.maximum(jnp.max(s_p, axis=-1, keepdims=True), jnp.max(s_n, axis=-1, keepdims=True))
        p_p = jnp.exp(s_p - m)
        p_n = jnp.exp(s_n - m)
        l = jnp.sum(p_p, axis=-1, keepdims=True) + jnp.sum(p_n, axis=-1, keepdims=True)
        acc = _dot(p_p.astype(BF16), vp_ref[...]) + _dot(p_n.astype(BF16), vn_ref[...])
        outs.append(acc / l)
    o_ref[...] = jnp.where(lane < V_HEAD, outs[0], outs[1]).astype(BF16)


def _attn_sample(q3, kp3, kn3, vp3, vn3):
    n_seq, s_len, _ = q3.shape
    past = kp3.shape[1]
    q_chunk = (past + np.arange(s_len)) // CHUNK
    assert (np.arange(past) // CHUNK).max() <= q_chunk.min()
    new_vis = q_chunk[None, :] <= q_chunk[:, None]
    assert new_vis.all(), "new-key mask would be needed"
    blk = lambda n, w: pl.BlockSpec((None, n, w), lambda b, p: (b, 0, p))
    return pl.pallas_call(
        functools.partial(_attn_sample_kernel, mask_new=None),
        grid=(n_seq, N_PAIRS),
        in_specs=[blk(s_len, 2 * SLOT), blk(past, 2 * SLOT), blk(s_len, 2 * SLOT),
                  blk(past, 2 * V_HEAD), blk(s_len, 2 * V_HEAD)],
        out_specs=blk(s_len, 2 * V_HEAD),
        out_shape=jax.ShapeDtypeStruct((n_seq, s_len, N_HEADS * V_HEAD), BF16),
        compiler_params=pltpu.CompilerParams(dimension_semantics=("arbitrary", "arbitrary"),
                                             vmem_limit_bytes=VMEM_LIMIT),
        name="attn_sample",
    )(q3, kp3, kn3, vp3, vn3)


def _post_kernel(x_ref, attn_ref, ta_ref, sgb_ref, gpost_ref, gfpre_ref, gfpost_ref,
                 wao_ref, wmg_ref, wgu_ref, wdn_ref, o_ref):
    yb = _dot(attn_ref[...], wao_ref[...])
    mixed = (ta_ref[...].astype(F32) + sgb_ref[...].astype(F32) * yb).astype(BF16)
    m = _dot(mixed, wmg_ref[...])
    x1 = x_ref[...] + _rms(m, gpost_ref[...])
    h2 = _rms(x1, gfpre_ref[...]).astype(BF16)
    g = _dot(h2, wgu_ref[:, :D_FF])
    up = _dot(h2, wgu_ref[:, D_FF:])
    act = (g * _sigmoid(g) * up).astype(BF16)
    f = _dot(act, wdn_ref[...])
    o_ref[...] = x1 + _rms(f, gfpost_ref[...])


def _post(x2d, attn2d, ta, sgb, lw, *, tm):
    n_tok = x2d.shape[0]
    tok = lambda w: pl.BlockSpec((tm, w), lambda i: (i, 0))
    return pl.pallas_call(
        _post_kernel,
        grid=(n_tok // tm,),
        in_specs=[tok(D_MODEL), tok(N_HEADS * V_HEAD), tok(D_MODEL), tok(D_MODEL),
                  _const_spec((1, D_MODEL)), _const_spec((1, D_MODEL)), _const_spec((1, D_MODEL)),
                  _const_spec(lw["w_attn_out"].shape), _const_spec(lw["w_merge"].shape),
                  _const_spec(lw["w_gate_up"].shape), _const_spec(lw["w_down"].shape)],
        out_specs=tok(D_MODEL),
        out_shape=jax.ShapeDtypeStruct((n_tok, D_MODEL), F32),
        compiler_params=pltpu.CompilerParams(dimension_semantics=("arbitrary",),
                                             vmem_limit_bytes=VMEM_LIMIT),
        name="post",
    )(x2d, attn2d, ta, sgb, lw["g_post"], lw["g_fpre"], lw["g_fpost"],
      lw["w_attn_out"], lw["w_merge"], lw["w_gate_up"], lw["w_down"])


def _rot_cols(w):
    half = QK_ROPE // 2
    return jnp.concatenate([-w[..., half:], w[..., :half]], axis=-1)


def _slot_pad(rope, nope):
    zeros = jnp.zeros(rope.shape[:-1] + (SLOT - QK_ROPE - QK_NOPE,), rope.dtype)
    return jnp.concatenate([rope, zeros, nope], axis=-1)


def _layer_weights(l, w_in, norm_attn_pre, norm_attn_post, norm_q, norm_kv, w_uq, w_ukv, conv_w,
                   w_conv_out, w_attn_out, w_merge, norm_ffn_pre, norm_ffn_post, w_gate_up, w_down):
    wi = w_in[l]
    o5 = 3 * D_CONV + Q_LORA + KV_LORA
    o6 = o5 + QK_ROPE
    w_kr = wi[:, o5:o6]
    zn = jnp.zeros((D_MODEL, QK_NOPE), F32)
    w_inp = jnp.concatenate([wi[:, :o5], _slot_pad(w_kr, zn), _slot_pad(_rot_cols(w_kr), zn),
                             wi[:, o6:]], axis=1).astype(BF16)
    assert w_inp.shape == (D_MODEL, D_INP)

    wq = w_uq[l].reshape(Q_LORA, N_HEADS, QK_NOPE + QK_ROPE)
    wq_n, wq_r = wq[..., :QK_NOPE], wq[..., QK_NOPE:]
    wq_plain = _slot_pad(wq_r, wq_n).reshape(Q_LORA, N_HEADS * SLOT)
    wq_rot = _slot_pad(_rot_cols(wq_r), jnp.zeros_like(wq_n)).reshape(Q_LORA, N_HEADS * SLOT)
    w_uqp = jnp.concatenate([wq_plain, wq_rot], axis=1).astype(BF16)

    wkv = w_ukv[l].reshape(KV_LORA, N_HEADS, QK_NOPE + V_HEAD)
    wk_n, wv = wkv[..., :QK_NOPE], wkv[..., QK_NOPE:]
    wk_slot = _slot_pad(jnp.zeros((KV_LORA, N_HEADS, QK_ROPE), F32), wk_n).reshape(KV_LORA, N_HEADS * SLOT)
    w_ukvp = jnp.concatenate([wk_slot, wv.reshape(KV_LORA, N_HEADS * V_HEAD)], axis=1).astype(BF16)

    place = jnp.eye(QK_ROPE, SLOT, dtype=BF16)
    row = lambda g: g[l].reshape(1, -1)
    return dict(
        w_in=w_inp, w_uq=w_uqp, w_ukv=w_ukvp, place=place,
        w_conv_out=w_conv_out[l].astype(BF16), w_attn_out=w_attn_out[l].astype(BF16),
        w_merge=w_merge[l].astype(BF16), w_gate_up=w_gate_up[l].astype(BF16),
        w_down=w_down[l].astype(BF16), conv_w=conv_w[l],
        g_pre=row(norm_attn_pre), g_post=row(norm_attn_post), g_q=row(norm_q), g_kv=row(norm_kv),
        g_fpre=row(norm_ffn_pre), g_fpost=row(norm_ffn_post))


def _rope_tables(n_pos):
    half = QK_ROPE // 2
    inv = ROPE_THETA ** (-jnp.arange(half, dtype=F32) / half)
    ang = jnp.arange(n_pos, dtype=jnp.int32).astype(F32)[:, None] * inv[None, :]
    cos, sin = jnp.cos(ang), jnp.sin(ang)
    ones = jnp.ones((n_pos, SLOT - QK_ROPE), F32)
    cos_t = jnp.concatenate([cos, cos, ones], axis=1)
    sin_t = jnp.concatenate([sin, sin, 0.0 * ones], axis=1)
    return cos_t, sin_t


def kernel(x_prompt, x_sample, state_conv, cache_ckv, cache_krope, w_in, norm_attn_pre,
           norm_attn_post, norm_q, norm_kv, w_uq, w_ukv, conv_w, w_conv_out, w_attn_out,
           w_merge, norm_ffn_pre, norm_ffn_post, w_gate_up, w_down):
    n_p, s_p, _ = x_prompt.shape
    n_s, s_s, _ = x_sample.shape
    depth, _, past, _ = cache_ckv.shape
    tm_p, tq = 256, 256
    assert past % s_s == 0
    cos_t, sin_t = _rope_tables(max(s_p, past + s_s))
    hist_p = jnp.zeros((n_p, CONV_W - 1, D_CONV), F32)

    xp = x_prompt.reshape(n_p * s_p, D_MODEL)
    xs = x_sample.reshape(n_s * s_s, D_MODEL)
    conv_p, ckv_p, kr_p, conv_s, ckv_s, kr_s = [], [], [], [], [], []
    for l in range(depth):
        lw = _layer_weights(l, w_in, norm_attn_pre, norm_attn_post, norm_q, norm_kv, w_uq, w_ukv,
                            conv_w, w_conv_out, w_attn_out, w_merge, norm_ffn_pre, norm_ffn_post,
                            w_gate_up, w_down)
        q, k, v, ta, sgb, c_new, kr_new, nc = _proj(xp, hist_p, cos_t, sin_t, 0, lw,
                                                    n_seq=n_p, seq_len=s_p, tm=tm_p)
        attn = _attn(q.reshape(n_p, s_p, -1), k.reshape(n_p, s_p, -1), v.reshape(n_p, s_p, -1), tq=tq)
        xp = _post(xp, attn.reshape(n_p * s_p, -1), ta, sgb, lw, tm=tm_p)
        conv_p.append(nc)
        ckv_p.append(c_new.reshape(n_p, s_p, KV_LORA))
        kr_p.append(kr_new.reshape(n_p, s_p, QK_ROPE))
        q, k, v, ta, sgb, c_new, kr_new, nc = _proj(xs, state_conv[l], cos_t, sin_t, past // s_s, lw,
                                                    n_seq=n_s, seq_len=s_s, tm=s_s)
        k_past, v_past = _kv_expand(cache_ckv[l].reshape(n_s * past, KV_LORA),
                                    cache_krope[l].reshape(n_s * past, QK_ROPE), lw, tm=512)
        attn = _attn_sample(q.reshape(n_s, s_s, -1), k_past.reshape(n_s, past, -1),
                            k.reshape(n_s, s_s, -1), v_past.reshape(n_s, past, -1),
                            v.reshape(n_s, s_s, -1))
        xs = _post(xs, attn.reshape(n_s * s_s, -1), ta, sgb, lw, tm=n_s * s_s)
        conv_s.append(nc)
        ckv_s.append(c_new.reshape(n_s, s_s, KV_LORA))
        kr_s.append(kr_new.reshape(n_s, s_s, QK_ROPE))

    return (xp.reshape(n_p, s_p, D_MODEL), xs.reshape(n_s, s_s, D_MODEL),
            jnp.stack(conv_p), jnp.stack(ckv_p), jnp.stack(kr_p),
            jnp.stack(conv_s), jnp.stack(ckv_s), jnp.stack(kr_s))
```

```python
import functools

import jax
import jax.numpy as jnp
import numpy as np
from jax import lax
from jax.experimental import pallas as pl
from jax.experimental.pallas import tpu as pltpu

D_MODEL = 1024
CHUNK = 64
D_CONV = 1024
CONV_W = 3
N_HEADS = 16
QK_NOPE = 64
QK_ROPE = 32
V_HEAD = 64
Q_LORA = 384
KV_LORA = 256
D_FF = 2816
ROPE_THETA = 10000.0
EPS = 1e-6
ATTN_SCALE = (QK_NOPE + QK_ROPE) ** -0.5
NEG_INF = -1e30

LANES = 128
SLOT = LANES
N_PAIRS = N_HEADS // 2
O_B, O_C, O_X = 0, D_CONV, 2 * D_CONV
O_QL = 3 * D_CONV
O_KV = O_QL + Q_LORA
O_KR = O_KV + KV_LORA
O_KRR = O_KR + SLOT
O_GA = O_KRR + SLOT
O_GB = O_GA + D_MODEL
D_INP = O_GB + D_MODEL
VMEM_LIMIT = 56 * 1024 * 1024

F32 = jnp.float32
BF16 = jnp.bfloat16


def _rms(x, g):
    return x * lax.rsqrt(jnp.mean(x * x, axis=-1, keepdims=True) + EPS) * g


def _sigmoid(x):
    return 1.0 / (1.0 + jnp.exp(-x))


def _dot(a, b):
    return jnp.dot(a, b, preferred_element_type=F32)


def _const_spec(shape):
    nd = len(shape)
    return pl.BlockSpec(shape, lambda *_: (0,) * nd, pipeline_mode=pl.Buffered(1))


def _proj_kernel(x_ref, hist_ref, cos_ref, sin_ref, gpre_ref, gq_ref, gkv_ref, cw_ref,
                 win_ref, wuq_ref, wukv_ref, wco_ref,
                 q_ref, k_ref, v_ref, ta_ref, sgb_ref, c_ref, kr_ref, nc_ref,
                 ubuf, *, tm, tiles_per_seq):
    i = pl.program_id(0)
    first = (i % tiles_per_seq) == 0

    @pl.when(first)
    def _():
        ubuf[6:8, :] = hist_ref[...]

    @pl.when(jnp.logical_not(first))
    def _():
        ubuf[0:8, :] = ubuf[tm:tm + 8, :]

    x = x_ref[...]
    h = _rms(x, gpre_ref[...]).astype(BF16)

    cx = _dot(h, win_ref[:, O_C:O_C + 2 * D_CONV])
    u = cx[:, :D_CONV] * cx[:, D_CONV:]
    ubuf[8:8 + tm, :] = u
    cw = cw_ref[...]
    conv = (cw[0:1, :] * ubuf[6:6 + tm, :] + cw[1:2, :] * ubuf[7:7 + tm, :] + cw[2:3, :] * u)
    bg = _dot(h, win_ref[:, O_B:O_B + D_CONV])
    ya = _dot((bg * conv).astype(BF16), wco_ref[...])
    ga = _dot(h, win_ref[:, O_GA:O_GA + D_MODEL])
    ta_ref[...] = (_sigmoid(ga) * ya).astype(BF16)
    gb = _dot(h, win_ref[:, O_GB:O_GB + D_MODEL])
    sgb_ref[...] = _sigmoid(gb).astype(BF16)
    nc_ref[...] = ubuf[tm + 6:tm + 8, :]

    cos = cos_ref[...]
    sin = sin_ref[...]
    small = _dot(h, win_ref[:, O_QL:O_GA])
    qn = _rms(small[:, :Q_LORA], gq_ref[...]).astype(BF16)
    c_new = _rms(small[:, Q_LORA:Q_LORA + KV_LORA], gkv_ref[...])
    c_ref[...] = c_new
    kr_slot = (small[:, O_KR - O_QL:O_KR - O_QL + SLOT] * cos
               + small[:, O_KRR - O_QL:O_KRR - O_QL + SLOT] * sin)
    kr_ref[...] = kr_slot[:, :QK_ROPE]

    qq = _dot(qn, wuq_ref[...])
    kvx = _dot(c_new.astype(BF16), wukv_ref[...])
    qw = N_HEADS * SLOT
    for hd in range(N_HEADS):
        a = qq[:, hd * SLOT:(hd + 1) * SLOT]
        b = qq[:, qw + hd * SLOT:qw + (hd + 1) * SLOT]
        q_ref[:, hd * SLOT:(hd + 1) * SLOT] = ((a * cos + b * sin) * ATTN_SCALE).astype(BF16)
        k_ref[:, hd * SLOT:(hd + 1) * SLOT] = (kvx[:, hd * SLOT:(hd + 1) * SLOT] + kr_slot).astype(BF16)
    v_ref[...] = kvx[:, qw:].astype(BF16)


def _proj(x2d, hist, cos_t, sin_t, pos_tile0, lw, *, n_seq, seq_len, tm):
    n_tok = n_seq * seq_len
    tiles_per_seq = seq_len // tm
    assert seq_len % tm == 0 and seq_len >= CONV_W - 1
    grid = (n_tok // tm,)
    tok = lambda w: pl.BlockSpec((tm, w), lambda i: (i, 0))
    pos = pl.BlockSpec((tm, SLOT), lambda i: (pos_tile0 + i % tiles_per_seq, 0))
    seq3 = pl.BlockSpec((None, CONV_W - 1, D_CONV), lambda i: (i // tiles_per_seq, 0, 0))
    out_shape = (
        jax.ShapeDtypeStruct((n_tok, N_HEADS * SLOT), BF16),
        jax.ShapeDtypeStruct((n_tok, N_HEADS * SLOT), BF16),
        jax.ShapeDtypeStruct((n_tok, N_HEADS * V_HEAD), BF16),
        jax.ShapeDtypeStruct((n_tok, D_MODEL), BF16),
        jax.ShapeDtypeStruct((n_tok, D_MODEL), BF16),
        jax.ShapeDtypeStruct((n_tok, KV_LORA), F32),
        jax.ShapeDtypeStruct((n_tok, QK_ROPE), F32),
        jax.ShapeDtypeStruct((n_seq, CONV_W - 1, D_CONV), F32),
    )
    return pl.pallas_call(
        functools.partial(_proj_kernel, tm=tm, tiles_per_seq=tiles_per_seq),
        grid=grid,
        in_specs=[tok(D_MODEL), seq3, pos, pos,
                  _const_spec((1, D_MODEL)), _const_spec((1, Q_LORA)), _const_spec((1, KV_LORA)),
                  _const_spec((CONV_W, D_CONV)),
                  _const_spec(lw["w_in"].shape), _const_spec(lw["w_uq"].shape),
                  _const_spec(lw["w_ukv"].shape), _const_spec(lw["w_conv_out"].shape)],
        out_specs=(tok(N_HEADS * SLOT), tok(N_HEADS * SLOT), tok(N_HEADS * V_HEAD),
                   tok(D_MODEL), tok(D_MODEL), tok(KV_LORA), tok(QK_ROPE), seq3),
        out_shape=out_shape,
        scratch_shapes=[pltpu.VMEM((tm + 8, D_CONV), F32)],
        compiler_params=pltpu.CompilerParams(dimension_semantics=("arbitrary",),
                                             vmem_limit_bytes=VMEM_LIMIT),
        name="proj",
    )(x2d, hist, cos_t, sin_t, lw["g_pre"], lw["g_q"], lw["g_kv"], lw["conv_w"],
      lw["w_in"], lw["w_uq"], lw["w_ukv"], lw["w_conv_out"])


def _flash_step(q, k, v, carry, mask):
    m, l, acc = carry
    s = lax.dot_general(q, k, (((1,), (1,)), ((), ())), preferred_element_type=F32)
    if mask is not None:
        s = jnp.where(mask, s, NEG_INF)
    m_new = jnp.maximum(m, jnp.max(s, axis=-1, keepdims=True))
    alpha = jnp.exp(m - m_new)
    p = jnp.exp(s - m_new)
    l = alpha * l + jnp.sum(p, axis=-1, keepdims=True)
    acc = alpha * acc + _dot(p.astype(BF16), v)
    return m_new, l, acc


def _attn_kernel(q_ref, k_ref, v_ref, o_ref, *, seq_len, tq):
    n_q = seq_len // tq
    row = lax.broadcasted_iota(jnp.int32, (tq, tq), 0)
    col = lax.broadcasted_iota(jnp.int32, (tq, tq), 1)
    diag_mask = (col // CHUNK) <= (row // CHUNK)
    lane = lax.broadcasted_iota(jnp.int32, (tq, 2 * V_HEAD), 1)
    for qi in range(n_q):
        outs = []
        for hh in range(2):
            q = q_ref[qi * tq:(qi + 1) * tq, hh * SLOT:(hh + 1) * SLOT]

            def body(j, carry, q=q, hh=hh):
                r0 = pl.multiple_of(j * tq, tq)
                k = k_ref[pl.ds(r0, tq), hh * SLOT:(hh + 1) * SLOT]
                v = v_ref[pl.ds(r0, tq), :]
                return _flash_step(q, k, v, carry, None)

            carry = (jnp.full((tq, 1), NEG_INF, F32), jnp.zeros((tq, 1), F32),
                     jnp.zeros((tq, 2 * V_HEAD), F32))
            for j in range(qi):
                carry = body(j, carry)
            k = k_ref[qi * tq:(qi + 1) * tq, hh * SLOT:(hh + 1) * SLOT]
            v = v_ref[qi * tq:(qi + 1) * tq, :]
            m, l, acc = _flash_step(q, k, v, carry, diag_mask)
            outs.append(acc / l)
        o_ref[qi * tq:(qi + 1) * tq, :] = jnp.where(lane < V_HEAD, outs[0], outs[1]).astype(BF16)


def _attn(q3, k3, v3, *, tq):
    n_seq, seq_len, _ = q3.shape
    assert seq_len % tq == 0 and tq % CHUNK == 0
    return pl.pallas_call(
        functools.partial(_attn_kernel, seq_len=seq_len, tq=tq),
        grid=(n_seq, N_PAIRS),
        in_specs=[pl.BlockSpec((None, seq_len, 2 * SLOT), lambda b, p: (b, 0, p)),
                  pl.BlockSpec((None, seq_len, 2 * SLOT), lambda b, p: (b, 0, p)),
                  pl.BlockSpec((None, seq_len, 2 * V_HEAD), lambda b, p: (b, 0, p))],
        out_specs=pl.BlockSpec((None, seq_len, 2 * V_HEAD), lambda b, p: (b, 0, p)),
        out_shape=jax.ShapeDtypeStruct((n_seq, seq_len, N_HEADS * V_HEAD), BF16),
        compiler_params=pltpu.CompilerParams(dimension_semantics=("arbitrary", "arbitrary"),
                                             vmem_limit_bytes=VMEM_LIMIT),
        name="attn",
    )(q3, k3, v3)


def _kv_expand_kernel(c_ref, kr_ref, wukv_ref, place_ref, k_ref, v_ref):
    kvx = _dot(c_ref[...].astype(BF16), wukv_ref[...])
    kr_slot = _dot(kr_ref[...].astype(BF16), place_ref[...])
    qw = N_HEADS * SLOT
    for hd in range(N_HEADS):
        k_ref[:, hd * SLOT:(hd + 1) * SLOT] = (kvx[:, hd * SLOT:(hd + 1) * SLOT] + kr_slot).astype(BF16)
    v_ref[...] = kvx[:, qw:].astype(BF16)


def _kv_expand(c2d, kr2d, lw, *, tm):
    n_tok = c2d.shape[0]
    tok = lambda w: pl.BlockSpec((tm, w), lambda i: (i, 0))
    return pl.pallas_call(
        _kv_expand_kernel,
        grid=(n_tok // tm,),
        in_specs=[tok(KV_LORA), tok(QK_ROPE), _const_spec(lw["w_ukv"].shape),
                  _const_spec(lw["place"].shape)],
        out_specs=(tok(N_HEADS * SLOT), tok(N_HEADS * V_HEAD)),
        out_shape=(jax.ShapeDtypeStruct((n_tok, N_HEADS * SLOT), BF16),
                   jax.ShapeDtypeStruct((n_tok, N_HEADS * V_HEAD), BF16)),
        compiler_params=pltpu.CompilerParams(dimension_semantics=("arbitrary",),
                                             vmem_limit_bytes=VMEM_LIMIT),
        name="kv_expand",
    )(c2d, kr2d, lw["w_ukv"], lw["place"])


def _attn_sample_kernel(q_ref, kp_ref, kn_ref, vp_ref, vn_ref, o_ref, *, mask_new):
    s_len = q_ref.shape[0]
    lane = lax.broadcasted_iota(jnp.int32, (s_len, 2 * V_HEAD), 1)
    outs = []
    for hh in range(2):
        sl = slice(hh * SLOT, (hh + 1) * SLOT)
        q = q_ref[:, sl]
        dn = (((1,), (1,)), ((), ()))
        s_p = lax.dot_general(q, kp_ref[:, sl], dn, preferred_element_type=F32)
        s_n = lax.dot_general(q, kn_ref[:, sl], dn, preferred_element_type=F32)
        if mask_new is not None:
            s_n = jnp.where(mask_new, s_n, NEG_INF)
        m = jnp.maximum(jnp.max(s_p, axis=-1, keepdims=True), jnp.max(s_n, axis=-1, keepdims=True))
        p_p = jnp.exp(s_p - m)
        p_n = jnp.exp(s_n - m)
        l = jnp.sum(p_p, axis=-1, keepdims=True) + jnp.sum(p_n, axis=-1, keepdims=True)
        acc = _dot(p_p.astype(BF16), vp_ref[...]) + _dot(p_n.astype(BF16), vn_ref[...])
        outs.append(acc / l)
    o_ref[...] = jnp.where(lane < V_HEAD, outs[0], outs[1]).astype(BF16)


def _attn_sample(q3, kp3, kn3, vp3, vn3):
    n_seq, s_len, _ = q3.shape
    past = kp3.shape[1]
    q_chunk = (past + np.arange(s_len)) // CHUNK
    assert (np.arange(past) // CHUNK).max() <= q_chunk.min()
    new_vis = q_chunk[None, :] <= q_chunk[:, None]
    assert new_vis.all(), "new-key mask would be needed"
    blk = lambda n, w: pl.BlockSpec((None, n, w), lambda b, p: (b, 0, p))
    return pl.pallas_call(
        functools.partial(_attn_sample_kernel, mask_new=None),
        grid=(n_seq, N_PAIRS),
        in_specs=[blk(s_len, 2 * SLOT), blk(past, 2 * SLOT), blk(s_len, 2 * SLOT),
                  blk(past, 2 * V_HEAD), blk(s_len, 2 * V_HEAD)],
        out_specs=blk(s_len, 2 * V_HEAD),
        out_shape=jax.ShapeDtypeStruct((n_seq, s_len, N_HEADS * V_HEAD), BF16),
        compiler_params=pltpu.CompilerParams(dimension_semantics=("arbitrary", "arbitrary"),
                                             vmem_limit_bytes=VMEM_LIMIT),
        name="attn_sample",
    )(q3, kp3, kn3, vp3, vn3)


def _post_kernel(x_ref, attn_ref, ta_ref, sgb_ref, gpost_ref, gfpre_ref, gfpost_ref,
                 wao_ref, wmg_ref, wgu_ref, wdn_ref, o_ref):
    yb = _dot(attn_ref[...], wao_ref[...])
    mixed = (ta_ref[...].astype(F32) + sgb_ref[...].astype(F32) * yb).astype(BF16)
    m = _dot(mixed, wmg_ref[...])
    x1 = x_ref[...] + _rms(m, gpost_ref[...])
    h2 = _rms(x1, gfpre_ref[...]).astype(BF16)
    g = _dot(h2, wgu_ref[:, :D_FF])
    up = _dot(h2, wgu_ref[:, D_FF:])
    act = (g * _sigmoid(g) * up).astype(BF16)
    f = _dot(act, wdn_ref[...])
    o_ref[...] = x1 + _rms(f, gfpost_ref[...])


def _post(x2d, attn2d, ta, sgb, lw, *, tm):
    n_tok = x2d.shape[0]
    tok = lambda w: pl.BlockSpec((tm, w), lambda i: (i, 0))
    return pl.pallas_call(
        _post_kernel,
        grid=(n_tok // tm,),
        in_specs=[tok(D_MODEL), tok(N_HEADS * V_HEAD), tok(D_MODEL), tok(D_MODEL),
                  _const_spec((1, D_MODEL)), _const_spec((1, D_MODEL)), _const_spec((1, D_MODEL)),
                  _const_spec(lw["w_attn_out"].shape), _const_spec(lw["w_merge"].shape),
                  _const_spec(lw["w_gate_up"].shape), _const_spec(lw["w_down"].shape)],
        out_specs=tok(D_MODEL),
        out_shape=jax.ShapeDtypeStruct((n_tok, D_MODEL), F32),
        compiler_params=pltpu.CompilerParams(dimension_semantics=("arbitrary",),
                                             vmem_limit_bytes=VMEM_LIMIT),
        name="post",
    )(x2d, attn2d, ta, sgb, lw["g_post"], lw["g_fpre"], lw["g_fpost"],
      lw["w_attn_out"], lw["w_merge"], lw["w_gate_up"], lw["w_down"])


def _rot_cols(w):
    half = QK_ROPE // 2
    return jnp.concatenate([-w[..., half:], w[..., :half]], axis=-1)


def _slot_pad(rope, nope):
    zeros = jnp.zeros(rope.shape[:-1] + (SLOT - QK_ROPE - QK_NOPE,), rope.dtype)
    return jnp.concatenate([rope, zeros, nope], axis=-1)


def _layer_weights(l, w_in, norm_attn_pre, norm_attn_post, norm_q, norm_kv, w_uq, w_ukv, conv_w,
                   w_conv_out, w_attn_out, w_merge, norm_ffn_pre, norm_ffn_post, w_gate_up, w_down):
    wi = w_in[l]
    o5 = 3 * D_CONV + Q_LORA + KV_LORA
    o6 = o5 + QK_ROPE
    w_kr = wi[:, o5:o6]
    zn = jnp.zeros((D_MODEL, QK_NOPE), F32)
    w_inp = jnp.concatenate([wi[:, :o5], _slot_pad(w_kr, zn), _slot_pad(_rot_cols(w_kr), zn),
                             wi[:, o6:]], axis=1).astype(BF16)
    assert w_inp.shape == (D_MODEL, D_INP)

    wq = w_uq[l].reshape(Q_LORA, N_HEADS, QK_NOPE + QK_ROPE)
    wq_n, wq_r = wq[..., :QK_NOPE], wq[..., QK_NOPE:]
    wq_plain = _slot_pad(wq_r, wq_n).reshape(Q_LORA, N_HEADS * SLOT)
    wq_rot = _slot_pad(_rot_cols(wq_r), jnp.zeros_like(wq_n)).reshape(Q_LORA, N_HEADS * SLOT)
    w_uqp = jnp.concatenate([wq_plain, wq_rot], axis=1).astype(BF16)

    wkv = w_ukv[l].reshape(KV_LORA, N_HEADS, QK_NOPE + V_HEAD)
    wk_n, wv = wkv[..., :QK_NOPE], wkv[..., QK_NOPE:]
    wk_slot = _slot_pad(jnp.zeros((KV_LORA, N_HEADS, QK_ROPE), F32), wk_n).reshape(KV_LORA, N_HEADS * SLOT)
    w_ukvp = jnp.concatenate([wk_slot, wv.reshape(KV_LORA, N_HEADS * V_HEAD)], axis=1).astype(BF16)

    place = jnp.eye(QK_ROPE, SLOT, dtype=BF16)
    row = lambda g: g[l].reshape(1, -1)
    return dict(
        w_in=w_inp, w_uq=w_uqp, w_ukv=w_ukvp, place=place,
        w_conv_out=w_conv_out[l].astype(BF16), w_attn_out=w_attn_out[l].astype(BF16),
        w_merge=w_merge[l].astype(BF16), w_gate_up=w_gate_up[l].astype(BF16),
        w_down=w_down[l].astype(BF16), conv_w=conv_w[l],
        g_pre=row(norm_attn_pre), g_post=row(norm_attn_post), g_q=row(norm_q), g_kv=row(norm_kv),
        g_fpre=row(norm_ffn_pre), g_fpost=row(norm_ffn_post))


def _rope_tables(n_pos):
    half = QK_ROPE // 2
    inv = ROPE_THETA ** (-jnp.arange(half, dtype=F32) / half)
    ang = jnp.arange(n_pos, dtype=jnp.int32).astype(F32)[:, None] * inv[None, :]
    cos, sin = jnp.cos(ang), jnp.sin(ang)
    ones = jnp.ones((n_pos, SLOT - QK_ROPE), F32)
    cos_t = jnp.concatenate([cos, cos, ones], axis=1)
    sin_t = jnp.concatenate([sin, sin, 0.0 * ones], axis=1)
    return cos_t, sin_t


def kernel(x_prompt, x_sample, state_conv, cache_ckv, cache_krope, w_in, norm_attn_pre,
           norm_attn_post, norm_q, norm_kv, w_uq, w_ukv, conv_w, w_conv_out, w_attn_out,
           w_merge, norm_ffn_pre, norm_ffn_post, w_gate_up, w_down):
    n_p, s_p, _ = x_prompt.shape
    n_s, s_s, _ = x_sample.shape
    depth, _, past, _ = cache_ckv.shape
    tm_proj, tm_post, tq = 256, 512, 256
    assert past % s_s == 0
    cos_t, sin_t = _rope_tables(max(s_p, past + s_s))
    hist_p = jnp.zeros((n_p, CONV_W - 1, D_CONV), F32)

    xp = x_prompt.reshape(n_p * s_p, D_MODEL)
    xs = x_sample.reshape(n_s * s_s, D_MODEL)
    conv_p, ckv_p, kr_p, conv_s, ckv_s, kr_s = [], [], [], [], [], []
    for l in range(depth):
        lw = _layer_weights(l, w_in, norm_attn_pre, norm_attn_post, norm_q, norm_kv, w_uq, w_ukv,
                            conv_w, w_conv_out, w_attn_out, w_merge, norm_ffn_pre, norm_ffn_post,
                            w_gate_up, w_down)
        q, k, v, ta, sgb, c_new, kr_new, nc = _proj(xp, hist_p, cos_t, sin_t, 0, lw,
                                                    n_seq=n_p, seq_len=s_p, tm=tm_proj)
        attn = _attn(q.reshape(n_p, s_p, -1), k.reshape(n_p, s_p, -1), v.reshape(n_p, s_p, -1), tq=tq)
        xp = _post(xp, attn.reshape(n_p * s_p, -1), ta, sgb, lw, tm=tm_post)
        conv_p.append(nc)
        ckv_p.append(c_new.reshape(n_p, s_p, KV_LORA))
        kr_p.append(kr_new.reshape(n_p, s_p, QK_ROPE))
        q, k, v, ta, sgb, c_new, kr_new, nc = _proj(xs, state_conv[l], cos_t, sin_t, past // s_s, lw,
                                                    n_seq=n_s, seq_len=s_s, tm=s_s)
        k_past, v_past = _kv_expand(cache_ckv[l].reshape(n_s * past, KV_LORA),
                                    cache_krope[l].reshape(n_s * past, QK_ROPE), lw, tm=512)
        attn = _attn_sample(q.reshape(n_s, s_s, -1), k_past.reshape(n_s, past, -1),
                            k.reshape(n_s, s_s, -1), v_past.reshape(n_s, past, -1),
                            v.reshape(n_s, s_s, -1))
        xs = _post(xs, attn.reshape(n_s * s_s, -1), ta, sgb, lw, tm=n_s * s_s)
        conv_s.append(nc)
        ckv_s.append(c_new.reshape(n_s, s_s, KV_LORA))
        kr_s.append(kr_new.reshape(n_s, s_s, QK_ROPE))

    return (xp.reshape(n_p, s_p, D_MODEL), xs.reshape(n_s, s_s, D_MODEL),
            jnp.stack(conv_p), jnp.stack(ckv_p), jnp.stack(kr_p),
            jnp.stack(conv_s), jnp.stack(ckv_s), jnp.stack(kr_s))
```

```python
import functools
import math

import jax
import jax.numpy as jnp
import numpy as np
from jax import lax
from jax.experimental import pallas as pl
from jax.experimental.pallas import tpu as pltpu

D_MODEL = 1024
CHUNK = 64
D_CONV = 1024
CONV_W = 3
N_HEADS = 16
QK_NOPE = 64
QK_ROPE = 32
V_HEAD = 64
Q_LORA = 384
KV_LORA = 256
D_FF = 2816
ROPE_THETA = 10000.0
EPS = 1e-6
ATTN_SCALE = (QK_NOPE + QK_ROPE) ** -0.5
NEG_INF = -1e30
Q_SCALE = ATTN_SCALE * math.log2(math.e)

LANES = 128
SLOT = LANES
N_PAIRS = N_HEADS // 2
O_B, O_C, O_X = 0, D_CONV, 2 * D_CONV
O_QL = 3 * D_CONV
O_KV = O_QL + Q_LORA
O_KR = O_KV + KV_LORA
O_KRR = O_KR + SLOT
O_GA = O_KRR + SLOT
O_GB = O_GA + D_MODEL
D_INP = O_GB + D_MODEL
VMEM_LIMIT = 56 * 1024 * 1024
ATTN_LOOKAHEAD = 3

F32 = jnp.float32
BF16 = jnp.bfloat16


def _rms(x, g):
    return x * lax.rsqrt(jnp.mean(x * x, axis=-1, keepdims=True) + EPS) * g


def _sigmoid(x):
    return 1.0 / (1.0 + jnp.exp(-x))


def _dot(a, b):
    return jnp.dot(a, b, preferred_element_type=F32)


def _const_spec(shape):
    nd = len(shape)
    return pl.BlockSpec(shape, lambda *_: (0,) * nd, pipeline_mode=pl.Buffered(1))


def _proj_kernel(x_ref, hist_ref, cos_ref, sin_ref, gpre_ref, gq_ref, gkv_ref, cw_ref,
                 win_ref, wuq_ref, wuk_ref, wuv_ref, wco_ref,
                 q_ref, k_ref, v_ref, ta_ref, sgb_ref, c_ref, kr_ref, nc_ref,
                 ubuf, *, tm, tiles_per_seq, transposed):
    i = pl.program_id(0)
    first = (i % tiles_per_seq) == 0

    @pl.when(first)
    def _():
        ubuf[6:8, :] = hist_ref[...]

    @pl.when(jnp.logical_not(first))
    def _():
        ubuf[0:8, :] = ubuf[tm:tm + 8, :]

    x = x_ref[...]
    h = _rms(x, gpre_ref[...]).astype(BF16)

    cx = _dot(h, win_ref[:, O_C:O_C + 2 * D_CONV])
    u = cx[:, :D_CONV] * cx[:, D_CONV:]
    ubuf[8:8 + tm, :] = u
    cw = cw_ref[...]
    conv = (cw[0:1, :] * ubuf[6:6 + tm, :] + cw[1:2, :] * ubuf[7:7 + tm, :] + cw[2:3, :] * u)
    bg = _dot(h, win_ref[:, O_B:O_B + D_CONV])
    ya = _dot((bg * conv).astype(BF16), wco_ref[...])
    ga = _dot(h, win_ref[:, O_GA:O_GA + D_MODEL])
    ta_ref[...] = (_sigmoid(ga) * ya).astype(BF16)
    gb = _dot(h, win_ref[:, O_GB:O_GB + D_MODEL])
    sgb_ref[...] = _sigmoid(gb).astype(BF16)
    nc_ref[...] = ubuf[tm + 6:tm + 8, :]

    cos = cos_ref[...]
    sin = sin_ref[...]
    small = _dot(h, win_ref[:, O_QL:O_GA])
    qn = _rms(small[:, :Q_LORA], gq_ref[...])
    c_new = _rms(small[:, Q_LORA:Q_LORA + KV_LORA], gkv_ref[...])
    c_ref[...] = c_new
    kr_slot = (small[:, O_KR - O_QL:O_KR - O_QL + SLOT] * cos
               + small[:, O_KRR - O_QL:O_KRR - O_QL + SLOT] * sin)
    kr_ref[...] = kr_slot[:, :QK_ROPE]

    c_bf = c_new.astype(BF16)
    kx = _dot(c_bf, wuk_ref[...])
    for hd in range(N_HEADS):
        sl = slice(hd * SLOT, (hd + 1) * SLOT)
        k_ref[:, sl] = (kx[:, sl] + kr_slot).astype(BF16)

    qw = N_HEADS * SLOT
    if transposed:
        cos_t, sin_t = cos.T, sin.T
        qq = _dot(wuq_ref[...], qn.T.astype(BF16))
        for hd in range(N_HEADS):
            a = qq[hd * SLOT:(hd + 1) * SLOT, :]
            b = qq[qw + hd * SLOT:qw + (hd + 1) * SLOT, :]
            q_ref[hd * SLOT:(hd + 1) * SLOT, :] = ((a * cos_t + b * sin_t) * Q_SCALE).astype(BF16)
        v_ref[...] = _dot(wuv_ref[...], c_new.T.astype(BF16)).astype(BF16)
    else:
        qq = _dot(qn.astype(BF16), wuq_ref[...])
        for hd in range(N_HEADS):
            sl = slice(hd * SLOT, (hd + 1) * SLOT)
            a = qq[:, sl]
            b = qq[:, qw + hd * SLOT:qw + (hd + 1) * SLOT]
            q_ref[:, sl] = ((a * cos + b * sin) * Q_SCALE).astype(BF16)
        v_ref[...] = _dot(c_bf, wuv_ref[...]).astype(BF16)


def _proj(x2d, hist, cos_t, sin_t, pos_tile0, lw, *, n_seq, seq_len, tm, transposed):
    n_tok = n_seq * seq_len
    tiles_per_seq = seq_len // tm
    assert seq_len % tm == 0 and seq_len >= CONV_W - 1
    grid = (n_tok // tm,)
    tok = lambda w: pl.BlockSpec((tm, w), lambda i: (i, 0))
    tok_t = lambda w: pl.BlockSpec((w, tm), lambda i: (0, i))
    pos = pl.BlockSpec((tm, SLOT), lambda i: (pos_tile0 + i % tiles_per_seq, 0))
    seq3 = pl.BlockSpec((None, CONV_W - 1, D_CONV), lambda i: (i // tiles_per_seq, 0, 0))
    qw, vw = N_HEADS * SLOT, N_HEADS * V_HEAD
    if transposed:
        q_shape, v_shape, q_spec, v_spec = (qw, n_tok), (vw, n_tok), tok_t(qw), tok_t(vw)
        w_uq, w_uv = lw["w_uq_t"], lw["w_uv_t"]
    else:
        q_shape, v_shape, q_spec, v_spec = (n_tok, qw), (n_tok, vw), tok(qw), tok(vw)
        w_uq, w_uv = lw["w_uq"], lw["w_uv"]
    out_shape = (
        jax.ShapeDtypeStruct(q_shape, BF16),
        jax.ShapeDtypeStruct((n_tok, qw), BF16),
        jax.ShapeDtypeStruct(v_shape, BF16),
        jax.ShapeDtypeStruct((n_tok, D_MODEL), BF16),
        jax.ShapeDtypeStruct((n_tok, D_MODEL), BF16),
        jax.ShapeDtypeStruct((n_tok, KV_LORA), F32),
        jax.ShapeDtypeStruct((n_tok, QK_ROPE), F32),
        jax.ShapeDtypeStruct((n_seq, CONV_W - 1, D_CONV), F32),
    )
    return pl.pallas_call(
        functools.partial(_proj_kernel, tm=tm, tiles_per_seq=tiles_per_seq, transposed=transposed),
        grid=grid,
        in_specs=[tok(D_MODEL), seq3, pos, pos,
                  _const_spec((1, D_MODEL)), _const_spec((1, Q_LORA)), _const_spec((1, KV_LORA)),
                  _const_spec((CONV_W, D_CONV)),
                  _const_spec(lw["w_in"].shape), _const_spec(w_uq.shape), _const_spec(lw["w_uk"].shape),
                  _const_spec(w_uv.shape), _const_spec(lw["w_conv_out"].shape)],
        out_specs=(q_spec, tok(qw), v_spec,
                   tok(D_MODEL), tok(D_MODEL), tok(KV_LORA), tok(QK_ROPE), seq3),
        out_shape=out_shape,
        scratch_shapes=[pltpu.VMEM((tm + 8, D_CONV), F32)],
        compiler_params=pltpu.CompilerParams(dimension_semantics=("arbitrary",),
                                             vmem_limit_bytes=VMEM_LIMIT),
        name="proj",
    )(x2d, hist, cos_t, sin_t, lw["g_pre"], lw["g_q"], lw["g_kv"], lw["conv_w"],
      lw["w_in"], w_uq, lw["w_uk"], w_uv, lw["w_conv_out"])


def _attn_kernel(qt_ref, k_ref, vt_ref, ot_ref, *, seq_len, tq):
    n_q = seq_len // tq
    key = lax.broadcasted_iota(jnp.int32, (tq, tq), 0)
    qry = lax.broadcasted_iota(jnp.int32, (tq, tq), 1)
    diag_mask = (key // CHUNK) <= (qry // CHUNK)
    def scores(qi, hh):
        q_t = qt_ref[hh * SLOT:(hh + 1) * SLOT, qi * tq:(qi + 1) * tq]
        hs = slice(hh * SLOT, (hh + 1) * SLOT)
        out = [_dot(k_ref[0:qi * tq, hs], q_t)] if qi > 0 else []
        out.append(jnp.where(diag_mask, _dot(k_ref[qi * tq:(qi + 1) * tq, hs], q_t), NEG_INF))
        return out

    def finish(qi, hh, s_parts):
        m = functools.reduce(jnp.maximum, [jnp.max(s, axis=0, keepdims=True) for s in s_parts])
        p_parts = [jnp.exp2(s - m) for s in s_parts]
        l = functools.reduce(jnp.add, [jnp.sum(p, axis=0, keepdims=True) for p in p_parts])
        vs = slice(hh * V_HEAD, (hh + 1) * V_HEAD)
        bounds = ([(0, qi * tq)] if qi > 0 else []) + [(qi * tq, (qi + 1) * tq)]
        acc = functools.reduce(jnp.add, [_dot(vt_ref[vs, a:b], p.astype(BF16))
                                         for (a, b), p in zip(bounds, p_parts)])
        ot_ref[vs, qi * tq:(qi + 1) * tq] = (acc * (1.0 / l)).astype(BF16)

    work = [(qi, hh) for qi in range(n_q) for hh in range(2)]
    pending = [scores(*w) for w in work[:ATTN_LOOKAHEAD]]
    for n, (qi, hh) in enumerate(work):
        if n + ATTN_LOOKAHEAD < len(work):
            pending.append(scores(*work[n + ATTN_LOOKAHEAD]))
        finish(qi, hh, pending.pop(0))


def _attn(q_t, k3, v_t, *, tq):
    n_seq, seq_len, _ = k3.shape
    assert seq_len % tq == 0 and tq % CHUNK == 0
    return pl.pallas_call(
        functools.partial(_attn_kernel, seq_len=seq_len, tq=tq),
        grid=(n_seq, N_PAIRS),
        in_specs=[pl.BlockSpec((2 * SLOT, seq_len), lambda b, p: (p, b)),
                  pl.BlockSpec((None, seq_len, 2 * SLOT), lambda b, p: (b, 0, p)),
                  pl.BlockSpec((2 * V_HEAD, seq_len), lambda b, p: (p, b))],
        out_specs=pl.BlockSpec((2 * V_HEAD, seq_len), lambda b, p: (p, b)),
        out_shape=jax.ShapeDtypeStruct((N_HEADS * V_HEAD, n_seq * seq_len), BF16),
        compiler_params=pltpu.CompilerParams(dimension_semantics=("arbitrary", "arbitrary"),
                                             vmem_limit_bytes=VMEM_LIMIT),
        name="attn",
    )(q_t, k3, v_t)


def _kv_expand_kernel(c_ref, kr_ref, wuk_ref, wuv_ref, place_ref, k_ref, v_ref):
    c_bf = c_ref[...].astype(BF16)
    kx = _dot(c_bf, wuk_ref[...])
    kr_slot = _dot(kr_ref[...].astype(BF16), place_ref[...])
    for hd in range(N_HEADS):
        sl = slice(hd * SLOT, (hd + 1) * SLOT)
        k_ref[:, sl] = (kx[:, sl] + kr_slot).astype(BF16)
    v_ref[...] = _dot(c_bf, wuv_ref[...]).astype(BF16)


def _kv_expand(c2d, kr2d, lw, *, tm):
    n_tok = c2d.shape[0]
    tok = lambda w: pl.BlockSpec((tm, w), lambda i: (i, 0))
    return pl.pallas_call(
        _kv_expand_kernel,
        grid=(n_tok // tm,),
        in_specs=[tok(KV_LORA), tok(QK_ROPE), _const_spec(lw["w_uk"].shape),
                  _const_spec(lw["w_uv"].shape), _const_spec(lw["place"].shape)],
        out_specs=(tok(N_HEADS * SLOT), tok(N_HEADS * V_HEAD)),
        out_shape=(jax.ShapeDtypeStruct((n_tok, N_HEADS * SLOT), BF16),
                   jax.ShapeDtypeStruct((n_tok, N_HEADS * V_HEAD), BF16)),
        compiler_params=pltpu.CompilerParams(dimension_semantics=("arbitrary",),
                                             vmem_limit_bytes=VMEM_LIMIT),
        name="kv_expand",
    )(c2d, kr2d, lw["w_uk"], lw["w_uv"], lw["place"])


def _attn_sample_kernel(q_ref, kp_ref, kn_ref, vp_ref, vn_ref, o_ref, *, mask_new):
    s_len = q_ref.shape[0]
    lane = lax.broadcasted_iota(jnp.int32, (s_len, 2 * V_HEAD), 1)
    outs = []
    for hh in range(2):
        sl = slice(hh * SLOT, (hh + 1) * SLOT)
        q = q_ref[:, sl]
        dn = (((1,), (1,)), ((), ()))
        s_p = lax.dot_general(q, kp_ref[:, sl], dn, preferred_element_type=F32)
        s_n = lax.dot_general(q, kn_ref[:, sl], dn, preferred_element_type=F32)
        if mask_new is not None:
            s_n = jnp.where(mask_new, s_n, NEG_INF)
        m = jnp.maximum(jnp.max(s_p, axis=-1, keepdims=True), jnp.max(s_n, axis=-1, keepdims=True))
        p_p = jnp.exp2(s_p - m)
        p_n = jnp.exp2(s_n - m)
        l = jnp.sum(p_p, axis=-1, keepdims=True) + jnp.sum(p_n, axis=-1, keepdims=True)
        acc = _dot(p_p.astype(BF16), vp_ref[...]) + _dot(p_n.astype(BF16), vn_ref[...])
        outs.append(acc / l)
    o_ref[...] = jnp.where(lane < V_HEAD, outs[0], outs[1]).astype(BF16)


def _attn_sample(q3, kp3, kn3, vp3, vn3):
    n_seq, s_len, _ = q3.shape
    past = kp3.shape[1]
    q_chunk = (past + np.arange(s_len)) // CHUNK
    assert (np.arange(past) // CHUNK).max() <= q_chunk.min()
    new_vis = q_chunk[None, :] <= q_chunk[:, None]
    assert new_vis.all(), "new-key mask would be needed"
    blk = lambda n, w: pl.BlockSpec((None, n, w), lambda b, p: (b, 0, p))
    return pl.pallas_call(
        functools.partial(_attn_sample_kernel, mask_new=None),
        grid=(n_seq, N_PAIRS),
        in_specs=[blk(s_len, 2 * SLOT), blk(past, 2 * SLOT), blk(s_len, 2 * SLOT),
                  blk(past, 2 * V_HEAD), blk(s_len, 2 * V_HEAD)],
        out_specs=blk(s_len, 2 * V_HEAD),
        out_shape=jax.ShapeDtypeStruct((n_seq, s_len, N_HEADS * V_HEAD), BF16),
        compiler_params=pltpu.CompilerParams(dimension_semantics=("arbitrary", "arbitrary"),
                                             vmem_limit_bytes=VMEM_LIMIT),
        name="attn_sample",
    )(q3, kp3, kn3, vp3, vn3)


def _post_kernel(x_ref, attn_ref, ta_ref, sgb_ref, gpost_ref, gfpre_ref, gfpost_ref,
                 wao_ref, wmg_ref, wgu_ref, wdn_ref, o_ref, *, transposed):
    if transposed:
        yb = lax.dot_general(attn_ref[...], wao_ref[...], (((0,), (0,)), ((), ())),
                             preferred_element_type=F32)
    else:
        yb = _dot(attn_ref[...], wao_ref[...])
    mixed = (ta_ref[...].astype(F32) + sgb_ref[...].astype(F32) * yb).astype(BF16)
    m = _dot(mixed, wmg_ref[...])
    x1 = x_ref[...] + _rms(m, gpost_ref[...])
    h2 = _rms(x1, gfpre_ref[...]).astype(BF16)
    g = _dot(h2, wgu_ref[:, :D_FF])
    up = _dot(h2, wgu_ref[:, D_FF:])
    act = (g * _sigmoid(g) * up).astype(BF16)
    f = _dot(act, wdn_ref[...])
    o_ref[...] = x1 + _rms(f, gfpost_ref[...])


def _post(x2d, attn2d, ta, sgb, lw, *, tm, transposed):
    n_tok = x2d.shape[0]
    tok = lambda w: pl.BlockSpec((tm, w), lambda i: (i, 0))
    vw = N_HEADS * V_HEAD
    attn_spec = pl.BlockSpec((vw, tm), lambda i: (0, i)) if transposed else tok(vw)
    return pl.pallas_call(
        functools.partial(_post_kernel, transposed=transposed),
        grid=(n_tok // tm,),
        in_specs=[tok(D_MODEL), attn_spec, tok(D_MODEL), tok(D_MODEL),
                  _const_spec((1, D_MODEL)), _const_spec((1, D_MODEL)), _const_spec((1, D_MODEL)),
                  _const_spec(lw["w_attn_out"].shape), _const_spec(lw["w_merge"].shape),
                  _const_spec(lw["w_gate_up"].shape), _const_spec(lw["w_down"].shape)],
        out_specs=tok(D_MODEL),
        out_shape=jax.ShapeDtypeStruct((n_tok, D_MODEL), F32),
        compiler_params=pltpu.CompilerParams(dimension_semantics=("arbitrary",),
                                             vmem_limit_bytes=VMEM_LIMIT),
        name="post",
    )(x2d, attn2d, ta, sgb, lw["g_post"], lw["g_fpre"], lw["g_fpost"],
      lw["w_attn_out"], lw["w_merge"], lw["w_gate_up"], lw["w_down"])


def _rot_cols(w):
    half = QK_ROPE // 2
    return jnp.concatenate([-w[..., half:], w[..., :half]], axis=-1)


def _slot_pad(rope, nope):
    zeros = jnp.zeros(rope.shape[:-1] + (SLOT - QK_ROPE - QK_NOPE,), rope.dtype)
    return jnp.concatenate([rope, zeros, nope], axis=-1)


def _layer_weights(l, w_in, norm_attn_pre, norm_attn_post, norm_q, norm_kv, w_uq, w_ukv, conv_w,
                   w_conv_out, w_attn_out, w_merge, norm_ffn_pre, norm_ffn_post, w_gate_up, w_down):
    wi = w_in[l]
    o5 = 3 * D_CONV + Q_LORA + KV_LORA
    o6 = o5 + QK_ROPE
    w_kr = wi[:, o5:o6]
    zn = jnp.zeros((D_MODEL, QK_NOPE), F32)
    w_inp = jnp.concatenate([wi[:, :o5], _slot_pad(w_kr, zn), _slot_pad(_rot_cols(w_kr), zn),
                             wi[:, o6:]], axis=1).astype(BF16)
    assert w_inp.shape == (D_MODEL, D_INP)

    wq = w_uq[l].reshape(Q_LORA, N_HEADS, QK_NOPE + QK_ROPE)
    wq_n, wq_r = wq[..., :QK_NOPE], wq[..., QK_NOPE:]
    wq_plain = _slot_pad(wq_r, wq_n).reshape(Q_LORA, N_HEADS * SLOT)
    wq_rot = _slot_pad(_rot_cols(wq_r), jnp.zeros_like(wq_n)).reshape(Q_LORA, N_HEADS * SLOT)
    w_uqp = jnp.concatenate([wq_plain, wq_rot], axis=1).astype(BF16)

    wkv = w_ukv[l].reshape(KV_LORA, N_HEADS, QK_NOPE + V_HEAD)
    wk_n, wv = wkv[..., :QK_NOPE], wkv[..., QK_NOPE:]
    wk_slot = _slot_pad(jnp.zeros((KV_LORA, N_HEADS, QK_ROPE), F32), wk_n).reshape(KV_LORA, N_HEADS * SLOT)
    w_uk = wk_slot.astype(BF16)
    w_uv = wv.reshape(KV_LORA, N_HEADS * V_HEAD).astype(BF16)

    place = jnp.eye(QK_ROPE, SLOT, dtype=BF16)
    row = lambda g: g[l].reshape(1, -1)
    return dict(
        w_in=w_inp, w_uq=w_uqp, w_uq_t=w_uqp.T, w_uk=w_uk, w_uv=w_uv, w_uv_t=w_uv.T, place=place,
        w_conv_out=w_conv_out[l].astype(BF16), w_attn_out=w_attn_out[l].astype(BF16),
        w_merge=w_merge[l].astype(BF16), w_gate_up=w_gate_up[l].astype(BF16),
        w_down=w_down[l].astype(BF16), conv_w=conv_w[l],
        g_pre=row(norm_attn_pre), g_post=row(norm_attn_post), g_q=row(norm_q), g_kv=row(norm_kv),
        g_fpre=row(norm_ffn_pre), g_fpost=row(norm_ffn_post))


def _rope_tables(n_pos):
    half = QK_ROPE // 2
    inv = ROPE_THETA ** (-jnp.arange(half, dtype=F32) / half)
    ang = jnp.arange(n_pos, dtype=jnp.int32).astype(F32)[:, None] * inv[None, :]
    cos, sin = jnp.cos(ang), jnp.sin(ang)
    ones = jnp.ones((n_pos, SLOT - QK_ROPE), F32)
    cos_t = jnp.concatenate([cos, cos, ones], axis=1)
    sin_t = jnp.concatenate([sin, sin, 0.0 * ones], axis=1)
    return cos_t, sin_t


def kernel(x_prompt, x_sample, state_conv, cache_ckv, cache_krope, w_in, norm_attn_pre,
           norm_attn_post, norm_q, norm_kv, w_uq, w_ukv, conv_w, w_conv_out, w_attn_out,
           w_merge, norm_ffn_pre, norm_ffn_post, w_gate_up, w_down):
    n_p, s_p, _ = x_prompt.shape
    n_s, s_s, _ = x_sample.shape
    depth, _, past, _ = cache_ckv.shape
    tm_proj, tm_post, tq = 256, 512, 256
    assert past % s_s == 0
    cos_t, sin_t = _rope_tables(max(s_p, past + s_s))
    hist_p = jnp.zeros((n_p, CONV_W - 1, D_CONV), F32)

    xp = x_prompt.reshape(n_p * s_p, D_MODEL)
    xs = x_sample.reshape(n_s * s_s, D_MODEL)
    conv_p, ckv_p, kr_p, conv_s, ckv_s, kr_s = [], [], [], [], [], []
    for l in range(depth):
        lw = _layer_weights(l, w_in, norm_attn_pre, norm_attn_post, norm_q, norm_kv, w_uq, w_ukv,
                            conv_w, w_conv_out, w_attn_out, w_merge, norm_ffn_pre, norm_ffn_post,
                            w_gate_up, w_down)
        q_t, k, v_t, ta, sgb, c_new, kr_new, nc = _proj(xp, hist_p, cos_t, sin_t, 0, lw, n_seq=n_p,
                                                        seq_len=s_p, tm=tm_proj, transposed=True)
        attn_t = _attn(q_t, k.reshape(n_p, s_p, -1), v_t, tq=tq)
        xp = _post(xp, attn_t, ta, sgb, lw, tm=tm_post, transposed=True)
        conv_p.append(nc)
        ckv_p.append(c_new.reshape(n_p, s_p, KV_LORA))
        kr_p.append(kr_new.reshape(n_p, s_p, QK_ROPE))
        q, k, v, ta, sgb, c_new, kr_new, nc = _proj(xs, state_conv[l], cos_t, sin_t, past // s_s, lw,
                                                    n_seq=n_s, seq_len=s_s, tm=s_s, transposed=False)
        k_past, v_past = _kv_expand(cache_ckv[l].reshape(n_s * past, KV_LORA),
                                    cache_krope[l].reshape(n_s * past, QK_ROPE), lw, tm=512)
        attn = _attn_sample(q.reshape(n_s, s_s, -1), k_past.reshape(n_s, past, -1),
                            k.reshape(n_s, s_s, -1), v_past.reshape(n_s, past, -1),
                            v.reshape(n_s, s_s, -1))
        xs = _post(xs, attn.reshape(n_s * s_s, -1), ta, sgb, lw, tm=n_s * s_s, transposed=False)
        conv_s.append(nc)
        ckv_s.append(c_new.reshape(n_s, s_s, KV_LORA))
        kr_s.append(kr_new.reshape(n_s, s_s, QK_ROPE))

    return (xp.reshape(n_p, s_p, D_MODEL), xs.reshape(n_s, s_s, D_MODEL),
            jnp.stack(conv_p), jnp.stack(ckv_p), jnp.stack(kr_p),
            jnp.stack(conv_s), jnp.stack(ckv_s), jnp.stack(kr_s))
```

```python
import functools
import math

import jax
import jax.numpy as jnp
import numpy as np
from jax import lax
from jax.experimental import pallas as pl
from jax.experimental.pallas import tpu as pltpu

D_MODEL = 1024
CHUNK = 64
D_CONV = 1024
CONV_W = 3
N_HEADS = 16
QK_NOPE = 64
QK_ROPE = 32
V_HEAD = 64
Q_LORA = 384
KV_LORA = 256
D_FF = 2816
ROPE_THETA = 10000.0
EPS = 1e-6
ATTN_SCALE = (QK_NOPE + QK_ROPE) ** -0.5
NEG_INF = -1e30
Q_SCALE = ATTN_SCALE * math.log2(math.e)

LANES = 128
SLOT = LANES
N_PAIRS = N_HEADS // 2
O_B, O_C, O_X = 0, D_CONV, 2 * D_CONV
O_QL = 3 * D_CONV
O_KV = O_QL + Q_LORA
O_KR = O_KV + KV_LORA
O_GATE = O_KR + QK_ROPE
assert O_KR % LANES == 0
VMEM_LIMIT = 56 * 1024 * 1024
ATTN_LOOKAHEAD = 3

F32 = jnp.float32
BF16 = jnp.bfloat16


def _rms(x, g):
    return x * lax.rsqrt(jnp.mean(x * x, axis=-1, keepdims=True) + EPS) * g


def _sigmoid(x):
    return 1.0 / (1.0 + jnp.exp(-x))


def _dot(a, b):
    return jnp.dot(a, b, preferred_element_type=F32)


def _const_spec(shape):
    nd = len(shape)
    return pl.BlockSpec(shape, lambda *_: (0,) * nd, pipeline_mode=pl.Buffered(1))


def _proj_kernel(x_ref, hist_ref, cos_ref, sin_ref, gpre_ref, gq_ref, gkv_ref, cw_ref,
                 win_ref, wkr_ref, wg_ref, wuq_ref, wuk_ref, wuv_ref, wco_ref,
                 q_ref, k_ref, v_ref, ta_ref, sgb_ref, c_ref, kr_ref, nc_ref,
                 ubuf, *, tm, tiles_per_seq, transposed):
    i = pl.program_id(0)
    first = (i % tiles_per_seq) == 0

    @pl.when(first)
    def _():
        ubuf[6:8, :] = hist_ref[...]

    @pl.when(jnp.logical_not(first))
    def _():
        ubuf[0:8, :] = ubuf[tm:tm + 8, :]

    x = x_ref[...]
    h = _rms(x, gpre_ref[...]).astype(BF16)

    cx = _dot(h, win_ref[:, O_C:O_C + 2 * D_CONV])
    u = cx[:, :D_CONV] * cx[:, D_CONV:]
    ubuf[8:8 + tm, :] = u
    cw = cw_ref[...]
    conv = (cw[0:1, :] * ubuf[6:6 + tm, :] + cw[1:2, :] * ubuf[7:7 + tm, :] + cw[2:3, :] * u)
    bg = _dot(h, win_ref[:, O_B:O_B + D_CONV])
    ya = _dot((bg * conv).astype(BF16), wco_ref[...])
    ga = _dot(h, wg_ref[:, :D_MODEL])
    ta_ref[...] = (_sigmoid(ga) * ya).astype(BF16)
    gb = _dot(h, wg_ref[:, D_MODEL:])
    sgb_ref[...] = _sigmoid(gb).astype(BF16)
    nc_ref[...] = ubuf[tm + 6:tm + 8, :]

    cos = cos_ref[...]
    sin = sin_ref[...]
    small = _dot(h, win_ref[:, O_QL:O_KR])
    qn = _rms(small[:, :Q_LORA], gq_ref[...])
    c_new = _rms(small[:, Q_LORA:], gkv_ref[...])
    c_ref[...] = c_new
    kr2 = _dot(h, wkr_ref[...])
    kr_slot = kr2[:, :SLOT] * cos + kr2[:, SLOT:] * sin
    kr_ref[...] = kr_slot[:, :QK_ROPE]

    c_bf = c_new.astype(BF16)
    kx = _dot(c_bf, wuk_ref[...])
    for hd in range(N_HEADS):
        sl = slice(hd * SLOT, (hd + 1) * SLOT)
        k_ref[:, sl] = (kx[:, sl] + kr_slot).astype(BF16)

    qw = N_HEADS * SLOT
    if transposed:
        half = QK_ROPE // 2
        cos_r, sin_r = cos.T[:QK_ROPE], sin.T[:QK_ROPE]
        qq = _dot(wuq_ref[...], qn.T.astype(BF16))
        for hd in range(N_HEADS):
            a = qq[hd * SLOT:(hd + 1) * SLOT, :]
            ar = a[:QK_ROPE]
            rot = jnp.concatenate([-ar[half:], ar[:half]], axis=0)
            q_ref[hd * SLOT:hd * SLOT + QK_ROPE, :] = ((ar * cos_r + rot * sin_r) * Q_SCALE).astype(BF16)
            q_ref[hd * SLOT + QK_ROPE:(hd + 1) * SLOT, :] = (a[QK_ROPE:] * Q_SCALE).astype(BF16)
        v_ref[...] = _dot(wuv_ref[...], c_new.T.astype(BF16)).astype(BF16)
    else:
        qq = _dot(qn.astype(BF16), wuq_ref[...])
        for hd in range(N_HEADS):
            sl = slice(hd * SLOT, (hd + 1) * SLOT)
            a = qq[:, sl]
            b = qq[:, qw + hd * SLOT:qw + (hd + 1) * SLOT]
            q_ref[:, sl] = ((a * cos + b * sin) * Q_SCALE).astype(BF16)
        v_ref[...] = _dot(c_bf, wuv_ref[...]).astype(BF16)


def _proj(x2d, hist, cos_t, sin_t, pos_tile0, lw, *, n_seq, seq_len, tm, transposed):
    n_tok = n_seq * seq_len
    tiles_per_seq = seq_len // tm
    assert seq_len % tm == 0 and seq_len >= CONV_W - 1
    grid = (n_tok // tm,)
    tok = lambda w: pl.BlockSpec((tm, w), lambda i: (i, 0))
    tok_t = lambda w: pl.BlockSpec((w, tm), lambda i: (0, i))
    pos = pl.BlockSpec((tm, SLOT), lambda i: (pos_tile0 + i % tiles_per_seq, 0))
    seq3 = pl.BlockSpec((None, CONV_W - 1, D_CONV), lambda i: (i // tiles_per_seq, 0, 0))
    qw, vw = N_HEADS * SLOT, N_HEADS * V_HEAD
    if transposed:
        q_shape, v_shape, q_spec, v_spec = (qw, n_tok), (vw, n_tok), tok_t(qw), tok_t(vw)
        w_uq, w_uv = lw["w_uq_t"], lw["w_uv_t"]
    else:
        q_shape, v_shape, q_spec, v_spec = (n_tok, qw), (n_tok, vw), tok(qw), tok(vw)
        w_uq, w_uv = lw["w_uq"], lw["w_uv"]
    out_shape = (
        jax.ShapeDtypeStruct(q_shape, BF16),
        jax.ShapeDtypeStruct((n_tok, qw), BF16),
        jax.ShapeDtypeStruct(v_shape, BF16),
        jax.ShapeDtypeStruct((n_tok, D_MODEL), BF16),
        jax.ShapeDtypeStruct((n_tok, D_MODEL), BF16),
        jax.ShapeDtypeStruct((n_tok, KV_LORA), F32),
        jax.ShapeDtypeStruct((n_tok, QK_ROPE), F32),
        jax.ShapeDtypeStruct((n_seq, CONV_W - 1, D_CONV), F32),
    )
    return pl.pallas_call(
        functools.partial(_proj_kernel, tm=tm, tiles_per_seq=tiles_per_seq, transposed=transposed),
        grid=grid,
        in_specs=[tok(D_MODEL), seq3, pos, pos,
                  _const_spec((1, D_MODEL)), _const_spec((1, Q_LORA)), _const_spec((1, KV_LORA)),
                  _const_spec((CONV_W, D_CONV)),
                  _const_spec(lw["w_in"].shape), _const_spec(lw["w_kr"].shape),
                  _const_spec(lw["w_gate"].shape), _const_spec(w_uq.shape), _const_spec(lw["w_uk"].shape),
                  _const_spec(w_uv.shape), _const_spec(lw["w_conv_out"].shape)],
        out_specs=(q_spec, tok(qw), v_spec,
                   tok(D_MODEL), tok(D_MODEL), tok(KV_LORA), tok(QK_ROPE), seq3),
        out_shape=out_shape,
        scratch_shapes=[pltpu.VMEM((tm + 8, D_CONV), F32)],
        compiler_params=pltpu.CompilerParams(dimension_semantics=("arbitrary",),
                                             vmem_limit_bytes=VMEM_LIMIT),
        name="proj",
    )(x2d, hist, cos_t, sin_t, lw["g_pre"], lw["g_q"], lw["g_kv"], lw["conv_w"],
      lw["w_in"], lw["w_kr"], lw["w_gate"], w_uq, lw["w_uk"], w_uv, lw["w_conv_out"])


def _attn_kernel(qt_ref, k_ref, vt_ref, ot_ref, *, seq_len, tq):
    n_q = seq_len // tq
    key = lax.broadcasted_iota(jnp.int32, (tq, tq), 0)
    qry = lax.broadcasted_iota(jnp.int32, (tq, tq), 1)
    diag_mask = (key // CHUNK) <= (qry // CHUNK)
    def scores(qi, hh):
        q_t = qt_ref[hh * SLOT:(hh + 1) * SLOT, qi * tq:(qi + 1) * tq]
        hs = slice(hh * SLOT, (hh + 1) * SLOT)
        out = [_dot(k_ref[0:qi * tq, hs], q_t)] if qi > 0 else []
        out.append(jnp.where(diag_mask, _dot(k_ref[qi * tq:(qi + 1) * tq, hs], q_t), NEG_INF))
        return out

    def finish(qi, hh, s_parts):
        m = functools.reduce(jnp.maximum, [jnp.max(s, axis=0, keepdims=True) for s in s_parts])
        p_parts = [jnp.exp2(s - m) for s in s_parts]
        l = functools.reduce(jnp.add, [jnp.sum(p, axis=0, keepdims=True) for p in p_parts])
        vs = slice(hh * V_HEAD, (hh + 1) * V_HEAD)
        bounds = ([(0, qi * tq)] if qi > 0 else []) + [(qi * tq, (qi + 1) * tq)]
        acc = functools.reduce(jnp.add, [_dot(vt_ref[vs, a:b], p.astype(BF16))
                                         for (a, b), p in zip(bounds, p_parts)])
        ot_ref[vs, qi * tq:(qi + 1) * tq] = (acc * (1.0 / l)).astype(BF16)

    work = [(qi, hh) for qi in range(n_q) for hh in range(2)]
    pending = [scores(*w) for w in work[:ATTN_LOOKAHEAD]]
    for n, (qi, hh) in enumerate(work):
        if n + ATTN_LOOKAHEAD < len(work):
            pending.append(scores(*work[n + ATTN_LOOKAHEAD]))
        finish(qi, hh, pending.pop(0))


def _attn(q_t, k3, v_t, *, tq):
    n_seq, seq_len, _ = k3.shape
    assert seq_len % tq == 0 and tq % CHUNK == 0
    return pl.pallas_call(
        functools.partial(_attn_kernel, seq_len=seq_len, tq=tq),
        grid=(n_seq, N_PAIRS),
        in_specs=[pl.BlockSpec((2 * SLOT, seq_len), lambda b, p: (p, b)),
                  pl.BlockSpec((None, seq_len, 2 * SLOT), lambda b, p: (b, 0, p)),
                  pl.BlockSpec((2 * V_HEAD, seq_len), lambda b, p: (p, b))],
        out_specs=pl.BlockSpec((2 * V_HEAD, seq_len), lambda b, p: (p, b)),
        out_shape=jax.ShapeDtypeStruct((N_HEADS * V_HEAD, n_seq * seq_len), BF16),
        compiler_params=pltpu.CompilerParams(dimension_semantics=("arbitrary", "arbitrary"),
                                             vmem_limit_bytes=VMEM_LIMIT),
        name="attn",
    )(q_t, k3, v_t)


def _kv_expand_kernel(c_ref, kr_ref, wuk_ref, wuv_ref, place_ref, k_ref, v_ref):
    c_bf = c_ref[...].astype(BF16)
    kx = _dot(c_bf, wuk_ref[...])
    kr_slot = _dot(kr_ref[...].astype(BF16), place_ref[...])
    for hd in range(N_HEADS):
        sl = slice(hd * SLOT, (hd + 1) * SLOT)
        k_ref[:, sl] = (kx[:, sl] + kr_slot).astype(BF16)
    v_ref[...] = _dot(c_bf, wuv_ref[...]).astype(BF16)


def _kv_expand(c2d, kr2d, lw, *, tm):
    n_tok = c2d.shape[0]
    tok = lambda w: pl.BlockSpec((tm, w), lambda i: (i, 0))
    return pl.pallas_call(
        _kv_expand_kernel,
        grid=(n_tok // tm,),
        in_specs=[tok(KV_LORA), tok(QK_ROPE), _const_spec(lw["w_uk"].shape),
                  _const_spec(lw["w_uv"].shape), _const_spec(lw["place"].shape)],
        out_specs=(tok(N_HEADS * SLOT), tok(N_HEADS * V_HEAD)),
        out_shape=(jax.ShapeDtypeStruct((n_tok, N_HEADS * SLOT), BF16),
                   jax.ShapeDtypeStruct((n_tok, N_HEADS * V_HEAD), BF16)),
        compiler_params=pltpu.CompilerParams(dimension_semantics=("arbitrary",),
                                             vmem_limit_bytes=VMEM_LIMIT),
        name="kv_expand",
    )(c2d, kr2d, lw["w_uk"], lw["w_uv"], lw["place"])


def _attn_sample_kernel(q_ref, kp_ref, kn_ref, vp_ref, vn_ref, o_ref, *, mask_new):
    s_len = q_ref.shape[0]
    lane = lax.broadcasted_iota(jnp.int32, (s_len, 2 * V_HEAD), 1)
    outs = []
    for hh in range(2):
        sl = slice(hh * SLOT, (hh + 1) * SLOT)
        q = q_ref[:, sl]
        dn = (((1,), (1,)), ((), ()))
        s_p = lax.dot_general(q, kp_ref[:, sl], dn, preferred_element_type=F32)
        s_n = lax.dot_general(q, kn_ref[:, sl], dn, preferred_element_type=F32)
        if mask_new is not None:
            s_n = jnp.where(mask_new, s_n, NEG_INF)
        m = jnp.maximum(jnp.max(s_p, axis=-1, keepdims=True), jnp.max(s_n, axis=-1, keepdims=True))
        p_p = jnp.exp2(s_p - m)
        p_n = jnp.exp2(s_n - m)
        l = jnp.sum(p_p, axis=-1, keepdims=True) + jnp.sum(p_n, axis=-1, keepdims=True)
        acc = _dot(p_p.astype(BF16), vp_ref[...]) + _dot(p_n.astype(BF16), vn_ref[...])
        outs.append(acc / l)
    o_ref[...] = jnp.where(lane < V_HEAD, outs[0], outs[1]).astype(BF16)


def _attn_sample(q3, kp3, kn3, vp3, vn3):
    n_seq, s_len, _ = q3.shape
    past = kp3.shape[1]
    q_chunk = (past + np.arange(s_len)) // CHUNK
    assert (np.arange(past) // CHUNK).max() <= q_chunk.min()
    new_vis = q_chunk[None, :] <= q_chunk[:, None]
    assert new_vis.all(), "new-key mask would be needed"
    blk = lambda n, w: pl.BlockSpec((None, n, w), lambda b, p: (b, 0, p))
    return pl.pallas_call(
        functools.partial(_attn_sample_kernel, mask_new=None),
        grid=(n_seq, N_PAIRS),
        in_specs=[blk(s_len, 2 * SLOT), blk(past, 2 * SLOT), blk(s_len, 2 * SLOT),
                  blk(past, 2 * V_HEAD), blk(s_len, 2 * V_HEAD)],
        out_specs=blk(s_len, 2 * V_HEAD),
        out_shape=jax.ShapeDtypeStruct((n_seq, s_len, N_HEADS * V_HEAD), BF16),
        compiler_params=pltpu.CompilerParams(dimension_semantics=("arbitrary", "arbitrary"),
                                             vmem_limit_bytes=VMEM_LIMIT),
        name="attn_sample",
    )(q3, kp3, kn3, vp3, vn3)


def _post_kernel(x_ref, attn_ref, ta_ref, sgb_ref, gpost_ref, gfpre_ref, gfpost_ref,
                 wao_ref, wmg_ref, wgu_ref, wdn_ref, o_ref, *, transposed):
    if transposed:
        yb = lax.dot_general(attn_ref[...], wao_ref[...], (((0,), (0,)), ((), ())),
                             preferred_element_type=F32)
    else:
        yb = _dot(attn_ref[...], wao_ref[...])
    mixed = (ta_ref[...].astype(F32) + sgb_ref[...].astype(F32) * yb).astype(BF16)
    m = _dot(mixed, wmg_ref[...])
    x1 = x_ref[...] + _rms(m, gpost_ref[...])
    h2 = _rms(x1, gfpre_ref[...]).astype(BF16)
    g = _dot(h2, wgu_ref[:, :D_FF])
    up = _dot(h2, wgu_ref[:, D_FF:])
    act = (g * _sigmoid(g) * up).astype(BF16)
    f = _dot(act, wdn_ref[...])
    o_ref[...] = x1 + _rms(f, gfpost_ref[...])


def _post(x2d, attn2d, ta, sgb, lw, *, tm, transposed):
    n_tok = x2d.shape[0]
    tok = lambda w: pl.BlockSpec((tm, w), lambda i: (i, 0))
    vw = N_HEADS * V_HEAD
    attn_spec = pl.BlockSpec((vw, tm), lambda i: (0, i)) if transposed else tok(vw)
    return pl.pallas_call(
        functools.partial(_post_kernel, transposed=transposed),
        grid=(n_tok // tm,),
        in_specs=[tok(D_MODEL), attn_spec, tok(D_MODEL), tok(D_MODEL),
                  _const_spec((1, D_MODEL)), _const_spec((1, D_MODEL)), _const_spec((1, D_MODEL)),
                  _const_spec(lw["w_attn_out"].shape), _const_spec(lw["w_merge"].shape),
                  _const_spec(lw["w_gate_up"].shape), _const_spec(lw["w_down"].shape)],
        out_specs=tok(D_MODEL),
        out_shape=jax.ShapeDtypeStruct((n_tok, D_MODEL), F32),
        compiler_params=pltpu.CompilerParams(dimension_semantics=("arbitrary",),
                                             vmem_limit_bytes=VMEM_LIMIT),
        name="post",
    )(x2d, attn2d, ta, sgb, lw["g_post"], lw["g_fpre"], lw["g_fpost"],
      lw["w_attn_out"], lw["w_merge"], lw["w_gate_up"], lw["w_down"])


def _rot_cols(w):
    half = QK_ROPE // 2
    return jnp.concatenate([-w[..., half:], w[..., :half]], axis=-1)


def _slot_pad(rope, nope):
    zeros = jnp.zeros(rope.shape[:-1] + (SLOT - QK_ROPE - QK_NOPE,), rope.dtype)
    return jnp.concatenate([rope, zeros, nope], axis=-1)


def _layer_weights(l, w_in, norm_attn_pre, norm_attn_post, norm_q, norm_kv, w_uq, w_ukv, conv_w,
                   w_conv_out, w_attn_out, w_merge, norm_ffn_pre, norm_ffn_post, w_gate_up, w_down):
    wi = w_in[l]
    w_main = wi[:, :O_KR].astype(BF16)
    w_gate = wi[:, O_GATE:].astype(BF16)
    w_kr = wi[:, O_KR:O_GATE].astype(BF16)
    zn = jnp.zeros((D_MODEL, QK_NOPE), BF16)
    w_kr2 = jnp.concatenate([_slot_pad(w_kr, zn), _slot_pad(_rot_cols(w_kr), zn)], axis=1)

    wq = w_uq[l].astype(BF16).reshape(Q_LORA, N_HEADS, QK_NOPE + QK_ROPE)
    wq_n, wq_r = wq[..., :QK_NOPE], wq[..., QK_NOPE:]
    wq_plain = _slot_pad(wq_r, wq_n).reshape(Q_LORA, N_HEADS * SLOT)
    wq_rot = _slot_pad(_rot_cols(wq_r), jnp.zeros_like(wq_n)).reshape(Q_LORA, N_HEADS * SLOT)
    w_uqp = jnp.concatenate([wq_plain, wq_rot], axis=1)

    wkv = w_ukv[l].astype(BF16).reshape(KV_LORA, N_HEADS, QK_NOPE + V_HEAD)
    wk_n, wv = wkv[..., :QK_NOPE], wkv[..., QK_NOPE:]
    w_uk = _slot_pad(jnp.zeros((KV_LORA, N_HEADS, QK_ROPE), BF16), wk_n).reshape(KV_LORA, N_HEADS * SLOT)
    w_uv = wv.reshape(KV_LORA, N_HEADS * V_HEAD)

    place = jnp.eye(QK_ROPE, SLOT, dtype=BF16)
    row = lambda g: g[l].reshape(1, -1)
    return dict(
        w_in=w_main, w_kr=w_kr2, w_gate=w_gate, w_uq=w_uqp, w_uq_t=wq_plain.T, w_uk=w_uk, w_uv=w_uv,
        w_uv_t=w_uv.T, place=place,
        w_conv_out=w_conv_out[l].astype(BF16), w_attn_out=w_attn_out[l].astype(BF16),
        w_merge=w_merge[l].astype(BF16), w_gate_up=w_gate_up[l].astype(BF16),
        w_down=w_down[l].astype(BF16), conv_w=conv_w[l],
        g_pre=row(norm_attn_pre), g_post=row(norm_attn_post), g_q=row(norm_q), g_kv=row(norm_kv),
        g_fpre=row(norm_ffn_pre), g_fpost=row(norm_ffn_post))


def _rope_tables(n_pos):
    half = QK_ROPE // 2
    inv = ROPE_THETA ** (-jnp.arange(half, dtype=F32) / half)
    ang = jnp.arange(n_pos, dtype=jnp.int32).astype(F32)[:, None] * inv[None, :]
    cos, sin = jnp.cos(ang), jnp.sin(ang)
    ones = jnp.ones((n_pos, SLOT - QK_ROPE), F32)
    cos_t = jnp.concatenate([cos, cos, ones], axis=1)
    sin_t = jnp.concatenate([sin, sin, 0.0 * ones], axis=1)
    return cos_t, sin_t


def kernel(x_prompt, x_sample, state_conv, cache_ckv, cache_krope, w_in, norm_attn_pre,
           norm_attn_post, norm_q, norm_kv, w_uq, w_ukv, conv_w, w_conv_out, w_attn_out,
           w_merge, norm_ffn_pre, norm_ffn_post, w_gate_up, w_down):
    n_p, s_p, _ = x_prompt.shape
    n_s, s_s, _ = x_sample.shape
    depth, _, past, _ = cache_ckv.shape
    tm_proj, tm_post, tq = 256, 512, 256
    assert past % s_s == 0
    cos_t, sin_t = _rope_tables(max(s_p, past + s_s))
    hist_p = jnp.zeros((n_p, CONV_W - 1, D_CONV), F32)

    xp = x_prompt.reshape(n_p * s_p, D_MODEL)
    xs = x_sample.reshape(n_s * s_s, D_MODEL)
    conv_p, ckv_p, kr_p, conv_s, ckv_s, kr_s = [], [], [], [], [], []
    for l in range(depth):
        lw = _layer_weights(l, w_in, norm_attn_pre, norm_attn_post, norm_q, norm_kv, w_uq, w_ukv,
                            conv_w, w_conv_out, w_attn_out, w_merge, norm_ffn_pre, norm_ffn_post,
                            w_gate_up, w_down)
        q_t, k, v_t, ta, sgb, c_new, kr_new, nc = _proj(xp, hist_p, cos_t, sin_t, 0, lw, n_seq=n_p,
                                                        seq_len=s_p, tm=tm_proj, transposed=True)
        attn_t = _attn(q_t, k.reshape(n_p, s_p, -1), v_t, tq=tq)
        xp = _post(xp, attn_t, ta, sgb, lw, tm=tm_post, transposed=True)
        conv_p.append(nc)
        ckv_p.append(c_new.reshape(n_p, s_p, KV_LORA))
        kr_p.append(kr_new.reshape(n_p, s_p, QK_ROPE))
        q, k, v, ta, sgb, c_new, kr_new, nc = _proj(xs, state_conv[l], cos_t, sin_t, past // s_s, lw,
                                                    n_seq=n_s, seq_len=s_s, tm=s_s, transposed=False)
        k_past, v_past = _kv_expand(cache_ckv[l].reshape(n_s * past, KV_LORA),
                                    cache_krope[l].reshape(n_s * past, QK_ROPE), lw, tm=512)
        attn = _attn_sample(q.reshape(n_s, s_s, -1), k_past.reshape(n_s, past, -1),
                            k.reshape(n_s, s_s, -1), v_past.reshape(n_s, past, -1),
                            v.reshape(n_s, s_s, -1))
        xs = _post(xs, attn.reshape(n_s * s_s, -1), ta, sgb, lw, tm=n_s * s_s, transposed=False)
        conv_s.append(nc)
        ckv_s.append(c_new.reshape(n_s, s_s, KV_LORA))
        kr_s.append(kr_new.reshape(n_s, s_s, QK_ROPE))

    return (xp.reshape(n_p, s_p, D_MODEL), xs.reshape(n_s, s_s, D_MODEL),
            jnp.stack(conv_p), jnp.stack(ckv_p), jnp.stack(kr_p),
            jnp.stack(conv_s), jnp.stack(ckv_s), jnp.stack(kr_s))
```

```python
import functools
import math

import jax
import jax.numpy as jnp
import numpy as np
from jax import lax
from jax.experimental import pallas as pl
from jax.experimental.pallas import tpu as pltpu

D_MODEL = 1024
CHUNK = 64
D_CONV = 1024
CONV_W = 3
N_HEADS = 16
QK_NOPE = 64
QK_ROPE = 32
V_HEAD = 64
Q_LORA = 384
KV_LORA = 256
D_FF = 2816
ROPE_THETA = 10000.0
EPS = 1e-6
ATTN_SCALE = (QK_NOPE + QK_ROPE) ** -0.5
NEG_INF = -1e30
Q_SCALE = ATTN_SCALE * math.log2(math.e)

LANES = 128
SLOT = LANES
N_PAIRS = N_HEADS // 2
O_B, O_C, O_X = 0, D_CONV, 2 * D_CONV
O_QL = 3 * D_CONV
O_KV = O_QL + Q_LORA
O_KR = O_KV + KV_LORA
O_GATE = O_KR + QK_ROPE
assert O_KR % LANES == 0
VMEM_LIMIT = 56 * 1024 * 1024
ATTN_LOOKAHEAD = 6

F32 = jnp.float32
BF16 = jnp.bfloat16


def _rms(x, g):
    return x * lax.rsqrt(jnp.mean(x * x, axis=-1, keepdims=True) + EPS) * g


def _sigmoid(x):
    return 1.0 / (1.0 + jnp.exp(-x))


def _dot(a, b):
    return jnp.dot(a, b, preferred_element_type=F32)


def _const_spec(shape):
    nd = len(shape)
    return pl.BlockSpec(shape, lambda *_: (0,) * nd, pipeline_mode=pl.Buffered(1))


def _proj_kernel(x_ref, hist_ref, cos_ref, sin_ref, gpre_ref, gq_ref, gkv_ref, cw_ref,
                 win_ref, wkr_ref, wg_ref, wuq_ref, wuk_ref, wuv_ref, wco_ref,
                 q_ref, k_ref, v_ref, ta_ref, sgb_ref, c_ref, kr_ref, nc_ref,
                 ubuf, *, tm, tiles_per_seq, seqs_per_tile, transposed):
    seg = tm // seqs_per_tile
    if seqs_per_tile == 1:
        first = (pl.program_id(0) % tiles_per_seq) == 0

        @pl.when(first)
        def _():
            ubuf[6:8, :] = hist_ref[0]

        @pl.when(jnp.logical_not(first))
        def _():
            ubuf[0:8, :] = ubuf[tm:tm + 8, :]

    x = x_ref[...]
    h = _rms(x, gpre_ref[...]).astype(BF16)
    cx = _dot(h, win_ref[:, O_C:O_C + 2 * D_CONV])
    u = cx[:, :D_CONV] * cx[:, D_CONV:]
    cw = cw_ref[...]
    conv_parts = []
    for sq in range(seqs_per_tile):
        u_seg = u[sq * seg:(sq + 1) * seg]
        if seqs_per_tile > 1:
            ubuf[6:8, :] = hist_ref[sq]
        ubuf[8:8 + seg, :] = u_seg
        conv_parts.append(cw[0:1, :] * ubuf[6:6 + seg, :] + cw[1:2, :] * ubuf[7:7 + seg, :]
                          + cw[2:3, :] * u_seg)
        nc_ref[sq] = ubuf[seg + 6:seg + 8, :]
    conv = conv_parts[0] if seqs_per_tile == 1 else jnp.concatenate(conv_parts, axis=0)
    bg = _dot(h, win_ref[:, O_B:O_B + D_CONV])
    ya = _dot((bg * conv).astype(BF16), wco_ref[...])
    ga = _dot(h, wg_ref[:, :D_MODEL])
    ta_ref[...] = (_sigmoid(ga) * ya).astype(BF16)
    gb = _dot(h, wg_ref[:, D_MODEL:])
    sgb_ref[...] = _sigmoid(gb).astype(BF16)

    cos = cos_ref[...]
    sin = sin_ref[...]
    small = _dot(h, win_ref[:, O_QL:O_KR])
    qn = _rms(small[:, :Q_LORA], gq_ref[...])
    c_new = _rms(small[:, Q_LORA:], gkv_ref[...])
    c_ref[...] = c_new
    kr2 = _dot(h, wkr_ref[...])
    kr_slot = kr2[:, :SLOT] * cos + kr2[:, SLOT:] * sin
    kr_ref[...] = kr_slot[:, :QK_ROPE]

    c_bf = c_new.astype(BF16)
    kx = _dot(c_bf, wuk_ref[...])
    for hd in range(N_HEADS):
        sl = slice(hd * SLOT, (hd + 1) * SLOT)
        k_ref[:, sl] = (kx[:, sl] + kr_slot).astype(BF16)

    qw = N_HEADS * SLOT
    if transposed:
        half = QK_ROPE // 2
        cos_r, sin_r = cos.T[:QK_ROPE], sin.T[:QK_ROPE]
        qq = _dot(wuq_ref[...], qn.T.astype(BF16))
        for hd in range(N_HEADS):
            a = qq[hd * SLOT:(hd + 1) * SLOT, :]
            ar = a[:QK_ROPE]
            rot = jnp.concatenate([-ar[half:], ar[:half]], axis=0)
            q_ref[hd * SLOT:hd * SLOT + QK_ROPE, :] = ((ar * cos_r + rot * sin_r) * Q_SCALE).astype(BF16)
            q_ref[hd * SLOT + QK_ROPE:(hd + 1) * SLOT, :] = (a[QK_ROPE:] * Q_SCALE).astype(BF16)
        v_ref[...] = _dot(wuv_ref[...], c_new.T.astype(BF16)).astype(BF16)
    else:
        qq = _dot(qn.astype(BF16), wuq_ref[...])
        for hd in range(N_HEADS):
            sl = slice(hd * SLOT, (hd + 1) * SLOT)
            a = qq[:, sl]
            b = qq[:, qw + hd * SLOT:qw + (hd + 1) * SLOT]
            q_ref[:, sl] = ((a * cos + b * sin) * Q_SCALE).astype(BF16)
        v_ref[...] = _dot(c_bf, wuv_ref[...]).astype(BF16)


def _proj(x2d, hist, cos_t, sin_t, pos_tile0, lw, *, n_seq, seq_len, tm, transposed):
    n_tok = n_seq * seq_len
    tiles_per_seq = max(seq_len // tm, 1)
    seqs_per_tile = max(tm // seq_len, 1)
    assert tiles_per_seq * tm == seq_len * seqs_per_tile and seq_len >= CONV_W - 1
    assert n_tok % tm == 0
    grid = (n_tok // tm,)
    tok = lambda w: pl.BlockSpec((tm, w), lambda i: (i, 0))
    tok_t = lambda w: pl.BlockSpec((w, tm), lambda i: (0, i))
    pos = pl.BlockSpec((tm, SLOT), lambda i: (pos_tile0 + i % tiles_per_seq, 0))
    seq3 = pl.BlockSpec((seqs_per_tile, CONV_W - 1, D_CONV), lambda i: (i // tiles_per_seq, 0, 0))
    qw, vw = N_HEADS * SLOT, N_HEADS * V_HEAD
    if transposed:
        q_shape, v_shape, q_spec, v_spec = (qw, n_tok), (vw, n_tok), tok_t(qw), tok_t(vw)
        w_uq, w_uv = lw["w_uq_t"], lw["w_uv_t"]
    else:
        q_shape, v_shape, q_spec, v_spec = (n_tok, qw), (n_tok, vw), tok(qw), tok(vw)
        w_uq, w_uv = lw["w_uq"], lw["w_uv"]
    out_shape = (
        jax.ShapeDtypeStruct(q_shape, BF16),
        jax.ShapeDtypeStruct((n_tok, qw), BF16),
        jax.ShapeDtypeStruct(v_shape, BF16),
        jax.ShapeDtypeStruct((n_tok, D_MODEL), BF16),
        jax.ShapeDtypeStruct((n_tok, D_MODEL), BF16),
        jax.ShapeDtypeStruct((n_tok, KV_LORA), F32),
        jax.ShapeDtypeStruct((n_tok, QK_ROPE), F32),
        jax.ShapeDtypeStruct((n_seq, CONV_W - 1, D_CONV), F32),
    )
    return pl.pallas_call(
        functools.partial(_proj_kernel, tm=tm, tiles_per_seq=tiles_per_seq,
                          seqs_per_tile=seqs_per_tile, transposed=transposed),
        grid=grid,
        in_specs=[tok(D_MODEL), seq3, pos, pos,
                  _const_spec((1, D_MODEL)), _const_spec((1, Q_LORA)), _const_spec((1, KV_LORA)),
                  _const_spec((CONV_W, D_CONV)),
                  _const_spec(lw["w_in"].shape), _const_spec(lw["w_kr"].shape),
                  _const_spec(lw["w_gate"].shape), _const_spec(w_uq.shape), _const_spec(lw["w_uk"].shape),
                  _const_spec(w_uv.shape), _const_spec(lw["w_conv_out"].shape)],
        out_specs=(q_spec, tok(qw), v_spec,
                   tok(D_MODEL), tok(D_MODEL), tok(KV_LORA), tok(QK_ROPE), seq3),
        out_shape=out_shape,
        scratch_shapes=[pltpu.VMEM((tm // seqs_per_tile + 8, D_CONV), F32)],
        compiler_params=pltpu.CompilerParams(dimension_semantics=("arbitrary",),
                                             vmem_limit_bytes=VMEM_LIMIT),
        name="proj",
    )(x2d, hist, cos_t, sin_t, lw["g_pre"], lw["g_q"], lw["g_kv"], lw["conv_w"],
      lw["w_in"], lw["w_kr"], lw["w_gate"], w_uq, lw["w_uk"], w_uv, lw["w_conv_out"])


def _attn_kernel(qt_ref, k_ref, vt_ref, ot_ref, *, seq_len, tq):
    n_q = seq_len // tq
    key = lax.broadcasted_iota(jnp.int32, (tq, tq), 0)
    qry = lax.broadcasted_iota(jnp.int32, (tq, tq), 1)
    diag_mask = (key // CHUNK) <= (qry // CHUNK)
    def scores(qi, hh):
        q_t = qt_ref[hh * SLOT:(hh + 1) * SLOT, qi * tq:(qi + 1) * tq]
        hs = slice(hh * SLOT, (hh + 1) * SLOT)
        out = [_dot(k_ref[0:qi * tq, hs], q_t)] if qi > 0 else []
        out.append(jnp.where(diag_mask, _dot(k_ref[qi * tq:(qi + 1) * tq, hs], q_t), NEG_INF))
        return out

    def finish(qi, hh, s_parts):
        m = functools.reduce(jnp.maximum, [jnp.max(s, axis=0, keepdims=True) for s in s_parts])
        p_parts = [jnp.exp2(s - m) for s in s_parts]
        l = functools.reduce(jnp.add, [jnp.sum(p, axis=0, keepdims=True) for p in p_parts])
        vs = slice(hh * V_HEAD, (hh + 1) * V_HEAD)
        bounds = ([(0, qi * tq)] if qi > 0 else []) + [(qi * tq, (qi + 1) * tq)]
        acc = functools.reduce(jnp.add, [_dot(vt_ref[vs, a:b], p.astype(BF16))
                                         for (a, b), p in zip(bounds, p_parts)])
        ot_ref[vs, qi * tq:(qi + 1) * tq] = (acc * (1.0 / l)).astype(BF16)

    work = [(qi, hh) for qi in range(n_q) for hh in range(2)]
    pending = [scores(*w) for w in work[:ATTN_LOOKAHEAD]]
    for n, (qi, hh) in enumerate(work):
        if n + ATTN_LOOKAHEAD < len(work):
            pending.append(scores(*work[n + ATTN_LOOKAHEAD]))
        finish(qi, hh, pending.pop(0))


def _attn(q_t, k3, v_t, *, tq):
    n_seq, seq_len, _ = k3.shape
    assert seq_len % tq == 0 and tq % CHUNK == 0
    return pl.pallas_call(
        functools.partial(_attn_kernel, seq_len=seq_len, tq=tq),
        grid=(n_seq, N_PAIRS),
        in_specs=[pl.BlockSpec((2 * SLOT, seq_len), lambda b, p: (p, b)),
                  pl.BlockSpec((None, seq_len, 2 * SLOT), lambda b, p: (b, 0, p)),
                  pl.BlockSpec((2 * V_HEAD, seq_len), lambda b, p: (p, b))],
        out_specs=pl.BlockSpec((2 * V_HEAD, seq_len), lambda b, p: (p, b)),
        out_shape=jax.ShapeDtypeStruct((N_HEADS * V_HEAD, n_seq * seq_len), BF16),
        compiler_params=pltpu.CompilerParams(dimension_semantics=("arbitrary", "arbitrary"),
                                             vmem_limit_bytes=VMEM_LIMIT),
        name="attn",
    )(q_t, k3, v_t)


def _kv_expand_kernel(c_ref, kr_ref, wuk_ref, wuv_ref, place_ref, k_ref, v_ref):
    c_bf = c_ref[...].astype(BF16)
    kx = _dot(c_bf, wuk_ref[...])
    kr_slot = _dot(kr_ref[...].astype(BF16), place_ref[...])
    for hd in range(N_HEADS):
        sl = slice(hd * SLOT, (hd + 1) * SLOT)
        k_ref[:, sl] = (kx[:, sl] + kr_slot).astype(BF16)
    v_ref[...] = _dot(c_bf, wuv_ref[...]).astype(BF16)


def _kv_expand(c2d, kr2d, layer, n_tok, lw, *, tm):
    assert n_tok % tm == 0
    blk0 = layer * (n_tok // tm)
    tok = lambda w: pl.BlockSpec((tm, w), lambda i: (i, 0))
    src = lambda w: pl.BlockSpec((tm, w), lambda i: (blk0 + i, 0))
    return pl.pallas_call(
        _kv_expand_kernel,
        grid=(n_tok // tm,),
        in_specs=[src(KV_LORA), src(QK_ROPE), _const_spec(lw["w_uk"].shape),
                  _const_spec(lw["w_uv"].shape), _const_spec(lw["place"].shape)],
        out_specs=(tok(N_HEADS * SLOT), tok(N_HEADS * V_HEAD)),
        out_shape=(jax.ShapeDtypeStruct((n_tok, N_HEADS * SLOT), BF16),
                   jax.ShapeDtypeStruct((n_tok, N_HEADS * V_HEAD), BF16)),
        compiler_params=pltpu.CompilerParams(dimension_semantics=("arbitrary",),
                                             vmem_limit_bytes=VMEM_LIMIT),
        name="kv_expand",
    )(c2d, kr2d, lw["w_uk"], lw["w_uv"], lw["place"])


def _attn_sample_kernel(q_ref, kp_ref, kn_ref, vp_ref, vn_ref, o_ref, *, mask_new):
    s_len = q_ref.shape[0]
    lane = lax.broadcasted_iota(jnp.int32, (s_len, 2 * V_HEAD), 1)
    outs = []
    for hh in range(2):
        sl = slice(hh * SLOT, (hh + 1) * SLOT)
        q = q_ref[:, sl]
        dn = (((1,), (1,)), ((), ()))
        s_p = lax.dot_general(q, kp_ref[:, sl], dn, preferred_element_type=F32)
        s_n = lax.dot_general(q, kn_ref[:, sl], dn, preferred_element_type=F32)
        if mask_new is not None:
            s_n = jnp.where(mask_new, s_n, NEG_INF)
        m = jnp.maximum(jnp.max(s_p, axis=-1, keepdims=True), jnp.max(s_n, axis=-1, keepdims=True))
        p_p = jnp.exp2(s_p - m)
        p_n = jnp.exp2(s_n - m)
        l = jnp.sum(p_p, axis=-1, keepdims=True) + jnp.sum(p_n, axis=-1, keepdims=True)
        acc = _dot(p_p.astype(BF16), vp_ref[...]) + _dot(p_n.astype(BF16), vn_ref[...])
        outs.append(acc / l)
    o_ref[...] = jnp.where(lane < V_HEAD, outs[0], outs[1]).astype(BF16)


def _attn_sample(q3, kp3, kn3, vp3, vn3):
    n_seq, s_len, _ = q3.shape
    past = kp3.shape[1]
    q_chunk = (past + np.arange(s_len)) // CHUNK
    assert (np.arange(past) // CHUNK).max() <= q_chunk.min()
    new_vis = q_chunk[None, :] <= q_chunk[:, None]
    assert new_vis.all(), "new-key mask would be needed"
    blk = lambda n, w: pl.BlockSpec((None, n, w), lambda b, p: (b, 0, p))
    return pl.pallas_call(
        functools.partial(_attn_sample_kernel, mask_new=None),
        grid=(n_seq, N_PAIRS),
        in_specs=[blk(s_len, 2 * SLOT), blk(past, 2 * SLOT), blk(s_len, 2 * SLOT),
                  blk(past, 2 * V_HEAD), blk(s_len, 2 * V_HEAD)],
        out_specs=blk(s_len, 2 * V_HEAD),
        out_shape=jax.ShapeDtypeStruct((n_seq, s_len, N_HEADS * V_HEAD), BF16),
        compiler_params=pltpu.CompilerParams(dimension_semantics=("arbitrary", "arbitrary"),
                                             vmem_limit_bytes=VMEM_LIMIT),
        name="attn_sample",
    )(q3, kp3, kn3, vp3, vn3)


def _post_kernel(x_ref, attn_ref, ta_ref, sgb_ref, gpost_ref, gfpre_ref, gfpost_ref,
                 wao_ref, wmg_ref, wgu_ref, wdn_ref, o_ref, *, transposed):
    if transposed:
        yb = lax.dot_general(attn_ref[...], wao_ref[...], (((0,), (0,)), ((), ())),
                             preferred_element_type=F32)
    else:
        yb = _dot(attn_ref[...], wao_ref[...])
    mixed = (ta_ref[...].astype(F32) + sgb_ref[...].astype(F32) * yb).astype(BF16)
    m = _dot(mixed, wmg_ref[...])
    x1 = x_ref[...] + _rms(m, gpost_ref[...])
    h2 = _rms(x1, gfpre_ref[...]).astype(BF16)
    g = _dot(h2, wgu_ref[:, :D_FF])
    up = _dot(h2, wgu_ref[:, D_FF:])
    act = (g * _sigmoid(g) * up).astype(BF16)
    f = _dot(act, wdn_ref[...])
    o_ref[...] = x1 + _rms(f, gfpost_ref[...])


def _post(x2d, attn2d, ta, sgb, lw, *, tm, transposed):
    n_tok = x2d.shape[0]
    tok = lambda w: pl.BlockSpec((tm, w), lambda i: (i, 0))
    vw = N_HEADS * V_HEAD
    attn_spec = pl.BlockSpec((vw, tm), lambda i: (0, i)) if transposed else tok(vw)
    return pl.pallas_call(
        functools.partial(_post_kernel, transposed=transposed),
        grid=(n_tok // tm,),
        in_specs=[tok(D_MODEL), attn_spec, tok(D_MODEL), tok(D_MODEL),
                  _const_spec((1, D_MODEL)), _const_spec((1, D_MODEL)), _const_spec((1, D_MODEL)),
                  _const_spec(lw["w_attn_out"].shape), _const_spec(lw["w_merge"].shape),
                  _const_spec(lw["w_gate_up"].shape), _const_spec(lw["w_down"].shape)],
        out_specs=tok(D_MODEL),
        out_shape=jax.ShapeDtypeStruct((n_tok, D_MODEL), F32),
        compiler_params=pltpu.CompilerParams(dimension_semantics=("arbitrary",),
                                             vmem_limit_bytes=VMEM_LIMIT),
        name="post",
    )(x2d, attn2d, ta, sgb, lw["g_post"], lw["g_fpre"], lw["g_fpost"],
      lw["w_attn_out"], lw["w_merge"], lw["w_gate_up"], lw["w_down"])


def _rot_cols(w):
    half = QK_ROPE // 2
    return jnp.concatenate([-w[..., half:], w[..., :half]], axis=-1)


def _slot_pad(rope, nope):
    zeros = jnp.zeros(rope.shape[:-1] + (SLOT - QK_ROPE - QK_NOPE,), rope.dtype)
    return jnp.concatenate([rope, zeros, nope], axis=-1)


def _layer_weights(l, w_in, norm_attn_pre, norm_attn_post, norm_q, norm_kv, w_uq, w_ukv, conv_w,
                   w_conv_out, w_attn_out, w_merge, norm_ffn_pre, norm_ffn_post, w_gate_up, w_down):
    wi = w_in[l]
    w_main = wi[:, :O_KR].astype(BF16)
    w_gate = wi[:, O_GATE:].astype(BF16)
    w_kr = wi[:, O_KR:O_GATE].astype(BF16)
    zn = jnp.zeros((D_MODEL, QK_NOPE), BF16)
    w_kr2 = jnp.concatenate([_slot_pad(w_kr, zn), _slot_pad(_rot_cols(w_kr), zn)], axis=1)

    wq = w_uq[l].astype(BF16).reshape(Q_LORA, N_HEADS, QK_NOPE + QK_ROPE)
    wq_n, wq_r = wq[..., :QK_NOPE], wq[..., QK_NOPE:]
    wq_plain = _slot_pad(wq_r, wq_n).reshape(Q_LORA, N_HEADS * SLOT)
    wq_rot = _slot_pad(_rot_cols(wq_r), jnp.zeros_like(wq_n)).reshape(Q_LORA, N_HEADS * SLOT)
    w_uqp = jnp.concatenate([wq_plain, wq_rot], axis=1)

    wkv = w_ukv[l].astype(BF16).reshape(KV_LORA, N_HEADS, QK_NOPE + V_HEAD)
    wk_n, wv = wkv[..., :QK_NOPE], wkv[..., QK_NOPE:]
    w_uk = _slot_pad(jnp.zeros((KV_LORA, N_HEADS, QK_ROPE), BF16), wk_n).reshape(KV_LORA, N_HEADS * SLOT)
    w_uv = wv.reshape(KV_LORA, N_HEADS * V_HEAD)

    place = jnp.eye(QK_ROPE, SLOT, dtype=BF16)
    row = lambda g: g[l].reshape(1, -1)
    return dict(
        w_in=w_main, w_kr=w_kr2, w_gate=w_gate, w_uq=w_uqp, w_uq_t=wq_plain.T, w_uk=w_uk, w_uv=w_uv,
        w_uv_t=w_uv.T, place=place,
        w_conv_out=w_conv_out[l].astype(BF16), w_attn_out=w_attn_out[l].astype(BF16),
        w_merge=w_merge[l].astype(BF16), w_gate_up=w_gate_up[l].astype(BF16),
        w_down=w_down[l].astype(BF16), conv_w=conv_w[l],
        g_pre=row(norm_attn_pre), g_post=row(norm_attn_post), g_q=row(norm_q), g_kv=row(norm_kv),
        g_fpre=row(norm_ffn_pre), g_fpost=row(norm_ffn_post))


def _rope_tables(n_pos):
    half = QK_ROPE // 2
    inv = ROPE_THETA ** (-jnp.arange(half, dtype=F32) / half)
    ang = jnp.arange(n_pos, dtype=jnp.int32).astype(F32)[:, None] * inv[None, :]
    cos, sin = jnp.cos(ang), jnp.sin(ang)
    ones = jnp.ones((n_pos, SLOT - QK_ROPE), F32)
    cos_t = jnp.concatenate([cos, cos, ones], axis=1)
    sin_t = jnp.concatenate([sin, sin, 0.0 * ones], axis=1)
    return cos_t, sin_t


def kernel(x_prompt, x_sample, state_conv, cache_ckv, cache_krope, w_in, norm_attn_pre,
           norm_attn_post, norm_q, norm_kv, w_uq, w_ukv, conv_w, w_conv_out, w_attn_out,
           w_merge, norm_ffn_pre, norm_ffn_post, w_gate_up, w_down):
    n_p, s_p, _ = x_prompt.shape
    n_s, s_s, _ = x_sample.shape
    depth, _, past, _ = cache_ckv.shape
    tm_proj, tm_post, tq = 256, 512, 256
    assert past % s_s == 0
    cos_t, sin_t = _rope_tables(max(s_p, past + s_s))
    cos_s = jnp.tile(cos_t[past:past + s_s], (n_s, 1))
    sin_s = jnp.tile(sin_t[past:past + s_s], (n_s, 1))
    hist_p = jnp.zeros((n_p, CONV_W - 1, D_CONV), F32)
    cache_c2d = cache_ckv.reshape(depth * n_s * past, KV_LORA)
    cache_kr2d = cache_krope.reshape(depth * n_s * past, QK_ROPE)

    xp = x_prompt.reshape(n_p * s_p, D_MODEL)
    xs = x_sample.reshape(n_s * s_s, D_MODEL)
    conv_p, ckv_p, kr_p, conv_s, ckv_s, kr_s = [], [], [], [], [], []
    for l in range(depth):
        lw = _layer_weights(l, w_in, norm_attn_pre, norm_attn_post, norm_q, norm_kv, w_uq, w_ukv,
                            conv_w, w_conv_out, w_attn_out, w_merge, norm_ffn_pre, norm_ffn_post,
                            w_gate_up, w_down)
        q_t, k, v_t, ta, sgb, c_new, kr_new, nc = _proj(xp, hist_p, cos_t, sin_t, 0, lw, n_seq=n_p,
                                                        seq_len=s_p, tm=tm_proj, transposed=True)
        attn_t = _attn(q_t, k.reshape(n_p, s_p, -1), v_t, tq=tq)
        xp = _post(xp, attn_t, ta, sgb, lw, tm=tm_post, transposed=True)
        conv_p.append(nc)
        ckv_p.append(c_new.reshape(n_p, s_p, KV_LORA))
        kr_p.append(kr_new.reshape(n_p, s_p, QK_ROPE))
        q, k, v, ta, sgb, c_new, kr_new, nc = _proj(xs, state_conv[l], cos_s, sin_s, 0, lw, n_seq=n_s,
                                                    seq_len=s_s, tm=n_s * s_s, transposed=False)
        k_past, v_past = _kv_expand(cache_c2d, cache_kr2d, l, n_s * past, lw, tm=512)
        attn = _attn_sample(q.reshape(n_s, s_s, -1), k_past.reshape(n_s, past, -1),
                            k.reshape(n_s, s_s, -1), v_past.reshape(n_s, past, -1),
                            v.reshape(n_s, s_s, -1))
        xs = _post(xs, attn.reshape(n_s * s_s, -1), ta, sgb, lw, tm=n_s * s_s, transposed=False)
        conv_s.append(nc)
        ckv_s.append(c_new.reshape(n_s, s_s, KV_LORA))
        kr_s.append(kr_new.reshape(n_s, s_s, QK_ROPE))

    return (xp.reshape(n_p, s_p, D_MODEL), xs.reshape(n_s, s_s, D_MODEL),
            jnp.stack(conv_p), jnp.stack(ckv_p), jnp.stack(kr_p),
            jnp.stack(conv_s), jnp.stack(ckv_s), jnp.stack(kr_s))
```

```python
import functools
import math

import jax
import jax.numpy as jnp
import numpy as np
from jax import lax
from jax.experimental import pallas as pl
from jax.experimental.pallas import tpu as pltpu

D_MODEL = 1024
CHUNK = 64
D_CONV = 1024
CONV_W = 3
N_HEADS = 16
QK_NOPE = 64
QK_ROPE = 32
V_HEAD = 64
Q_LORA = 384
KV_LORA = 256
D_FF = 2816
ROPE_THETA = 10000.0
EPS = 1e-6
ATTN_SCALE = (QK_NOPE + QK_ROPE) ** -0.5
NEG_INF = -1e30
Q_SCALE = ATTN_SCALE * math.log2(math.e)

LANES = 128
SLOT = LANES
N_PAIRS = N_HEADS // 2
O_B, O_C, O_X = 0, D_CONV, 2 * D_CONV
O_QL = 3 * D_CONV
O_KV = O_QL + Q_LORA
O_KR = O_KV + KV_LORA
O_GATE = O_KR + QK_ROPE
assert O_KR % LANES == 0
VMEM_LIMIT = 56 * 1024 * 1024
ATTN_LOOKAHEAD = 8

F32 = jnp.float32
BF16 = jnp.bfloat16


def _rms(x, g):
    return x * lax.rsqrt(jnp.mean(x * x, axis=-1, keepdims=True) + EPS) * g


def _sigmoid(x):
    return 1.0 / (1.0 + jnp.exp(-x))


def _dot(a, b):
    return jnp.dot(a, b, preferred_element_type=F32)


def _const_spec(shape):
    nd = len(shape)
    return pl.BlockSpec(shape, lambda *_: (0,) * nd, pipeline_mode=pl.Buffered(1))


def _proj_kernel(x_ref, hist_ref, cos_ref, sin_ref, gpre_ref, gq_ref, gkv_ref, cw_ref,
                 win_ref, wkr_ref, wg_ref, wuq_ref, wuk_ref, wuv_ref, wco_ref,
                 q_ref, k_ref, v_ref, ta_ref, sgb_ref, c_ref, kr_ref, nc_ref,
                 ubuf, *, tm, tiles_per_seq, seqs_per_tile, transposed):
    seg = tm // seqs_per_tile
    if seqs_per_tile == 1:
        first = (pl.program_id(0) % tiles_per_seq) == 0

        @pl.when(first)
        def _():
            ubuf[6:8, :] = hist_ref[0]

        @pl.when(jnp.logical_not(first))
        def _():
            ubuf[0:8, :] = ubuf[tm:tm + 8, :]

    x = x_ref[...]
    h = _rms(x, gpre_ref[...]).astype(BF16)
    cx = _dot(h, win_ref[:, O_C:O_C + 2 * D_CONV])
    u = cx[:, :D_CONV] * cx[:, D_CONV:]
    cw = cw_ref[...]
    conv_parts = []
    for sq in range(seqs_per_tile):
        u_seg = u[sq * seg:(sq + 1) * seg]
        if seqs_per_tile > 1:
            ubuf[6:8, :] = hist_ref[sq]
        ubuf[8:8 + seg, :] = u_seg
        conv_parts.append(cw[0:1, :] * ubuf[6:6 + seg, :] + cw[1:2, :] * ubuf[7:7 + seg, :]
                          + cw[2:3, :] * u_seg)
        nc_ref[sq] = ubuf[seg + 6:seg + 8, :]
    conv = conv_parts[0] if seqs_per_tile == 1 else jnp.concatenate(conv_parts, axis=0)
    bg = _dot(h, win_ref[:, O_B:O_B + D_CONV])
    ya = _dot((bg * conv).astype(BF16), wco_ref[...])
    ga = _dot(h, wg_ref[:, :D_MODEL])
    ta_ref[...] = (_sigmoid(ga) * ya).astype(BF16)
    gb = _dot(h, wg_ref[:, D_MODEL:])
    sgb_ref[...] = _sigmoid(gb).astype(BF16)

    cos = cos_ref[...]
    sin = sin_ref[...]
    small = _dot(h, win_ref[:, O_QL:O_KR])
    qn = _rms(small[:, :Q_LORA], gq_ref[...])
    c_new = _rms(small[:, Q_LORA:], gkv_ref[...])
    c_ref[...] = c_new
    kr2 = _dot(h, wkr_ref[...])
    kr_slot = kr2[:, :SLOT] * cos + kr2[:, SLOT:] * sin
    kr_ref[...] = kr_slot[:, :QK_ROPE]

    c_bf = c_new.astype(BF16)
    kx = _dot(c_bf, wuk_ref[...])
    for hd in range(N_HEADS):
        sl = slice(hd * SLOT, (hd + 1) * SLOT)
        k_ref[:, sl] = (kx[:, sl] + kr_slot).astype(BF16)

    qw = N_HEADS * SLOT
    if transposed:
        half = QK_ROPE // 2
        cos_r, sin_r = cos.T[:QK_ROPE], sin.T[:QK_ROPE]
        qq = _dot(wuq_ref[...], qn.T.astype(BF16))
        for hd in range(N_HEADS):
            a = qq[hd * SLOT:(hd + 1) * SLOT, :]
            ar = a[:QK_ROPE]
            rot = jnp.concatenate([-ar[half:], ar[:half]], axis=0)
            q_ref[hd * SLOT:hd * SLOT + QK_ROPE, :] = ((ar * cos_r + rot * sin_r) * Q_SCALE).astype(BF16)
            q_ref[hd * SLOT + QK_ROPE:(hd + 1) * SLOT, :] = (a[QK_ROPE:] * Q_SCALE).astype(BF16)
        v_ref[...] = _dot(wuv_ref[...], c_new.T.astype(BF16)).astype(BF16)
    else:
        qq = _dot(qn.astype(BF16), wuq_ref[...])
        for hd in range(N_HEADS):
            sl = slice(hd * SLOT, (hd + 1) * SLOT)
            a = qq[:, sl]
            b = qq[:, qw + hd * SLOT:qw + (hd + 1) * SLOT]
            q_ref[:, sl] = ((a * cos + b * sin) * Q_SCALE).astype(BF16)
        v_ref[...] = _dot(c_bf, wuv_ref[...]).astype(BF16)


def _proj(x2d, hist, cos_t, sin_t, pos_tile0, lw, *, n_seq, seq_len, tm, transposed):
    n_tok = n_seq * seq_len
    tiles_per_seq = max(seq_len // tm, 1)
    seqs_per_tile = max(tm // seq_len, 1)
    assert tiles_per_seq * tm == seq_len * seqs_per_tile and seq_len >= CONV_W - 1
    assert n_tok % tm == 0
    grid = (n_tok // tm,)
    tok = lambda w: pl.BlockSpec((tm, w), lambda i: (i, 0))
    tok_t = lambda w: pl.BlockSpec((w, tm), lambda i: (0, i))
    pos = pl.BlockSpec((tm, SLOT), lambda i: (pos_tile0 + i % tiles_per_seq, 0))
    seq3 = pl.BlockSpec((seqs_per_tile, CONV_W - 1, D_CONV), lambda i: (i // tiles_per_seq, 0, 0))
    qw, vw = N_HEADS * SLOT, N_HEADS * V_HEAD
    if transposed:
        q_shape, v_shape, q_spec, v_spec = (qw, n_tok), (vw, n_tok), tok_t(qw), tok_t(vw)
        w_uq, w_uv = lw["w_uq_t"], lw["w_uv_t"]
    else:
        q_shape, v_shape, q_spec, v_spec = (n_tok, qw), (n_tok, vw), tok(qw), tok(vw)
        w_uq, w_uv = lw["w_uq"], lw["w_uv"]
    out_shape = (
        jax.ShapeDtypeStruct(q_shape, BF16),
        jax.ShapeDtypeStruct((n_tok, qw), BF16),
        jax.ShapeDtypeStruct(v_shape, BF16),
        jax.ShapeDtypeStruct((n_tok, D_MODEL), BF16),
        jax.ShapeDtypeStruct((n_tok, D_MODEL), BF16),
        jax.ShapeDtypeStruct((n_tok, KV_LORA), F32),
        jax.ShapeDtypeStruct((n_tok, QK_ROPE), F32),
        jax.ShapeDtypeStruct((n_seq, CONV_W - 1, D_CONV), F32),
    )
    return pl.pallas_call(
        functools.partial(_proj_kernel, tm=tm, tiles_per_seq=tiles_per_seq,
                          seqs_per_tile=seqs_per_tile, transposed=transposed),
        grid=grid,
        in_specs=[tok(D_MODEL), seq3, pos, pos,
                  _const_spec((1, D_MODEL)), _const_spec((1, Q_LORA)), _const_spec((1, KV_LORA)),
                  _const_spec((CONV_W, D_CONV)),
                  _const_spec(lw["w_in"].shape), _const_spec(lw["w_kr"].shape),
                  _const_spec(lw["w_gate"].shape), _const_spec(w_uq.shape), _const_spec(lw["w_uk"].shape),
                  _const_spec(w_uv.shape), _const_spec(lw["w_conv_out"].shape)],
        out_specs=(q_spec, tok(qw), v_spec,
                   tok(D_MODEL), tok(D_MODEL), tok(KV_LORA), tok(QK_ROPE), seq3),
        out_shape=out_shape,
        scratch_shapes=[pltpu.VMEM((tm // seqs_per_tile + 8, D_CONV), F32)],
        compiler_params=pltpu.CompilerParams(dimension_semantics=("arbitrary",),
                                             vmem_limit_bytes=VMEM_LIMIT),
        name="proj",
    )(x2d, hist, cos_t, sin_t, lw["g_pre"], lw["g_q"], lw["g_kv"], lw["conv_w"],
      lw["w_in"], lw["w_kr"], lw["w_gate"], w_uq, lw["w_uk"], w_uv, lw["w_conv_out"])


def _attn_kernel(qt_ref, k_ref, vt_ref, ot_ref, *, seq_len, tq):
    n_q = seq_len // tq
    key = lax.broadcasted_iota(jnp.int32, (tq, tq), 0)
    qry = lax.broadcasted_iota(jnp.int32, (tq, tq), 1)
    diag_mask = (key // CHUNK) <= (qry // CHUNK)
    def scores(qi, hh):
        q_t = qt_ref[hh * SLOT:(hh + 1) * SLOT, qi * tq:(qi + 1) * tq]
        hs = slice(hh * SLOT, (hh + 1) * SLOT)
        out = [_dot(k_ref[0:qi * tq, hs], q_t)] if qi > 0 else []
        out.append(jnp.where(diag_mask, _dot(k_ref[qi * tq:(qi + 1) * tq, hs], q_t), NEG_INF))
        return out

    def finish(qi, hh, s_parts):
        m = functools.reduce(jnp.maximum, [jnp.max(s, axis=0, keepdims=True) for s in s_parts])
        p_parts = [jnp.exp2(s - m) for s in s_parts]
        l = functools.reduce(jnp.add, [jnp.sum(p, axis=0, keepdims=True) for p in p_parts])
        vs = slice(hh * V_HEAD, (hh + 1) * V_HEAD)
        bounds = ([(0, qi * tq)] if qi > 0 else []) + [(qi * tq, (qi + 1) * tq)]
        acc = functools.reduce(jnp.add, [_dot(vt_ref[vs, a:b], p.astype(BF16))
                                         for (a, b), p in zip(bounds, p_parts)])
        ot_ref[vs, qi * tq:(qi + 1) * tq] = (acc * (1.0 / l)).astype(BF16)

    work = [(qi, hh) for qi in range(n_q) for hh in range(2)]
    pending = [scores(*w) for w in work[:ATTN_LOOKAHEAD]]
    for n, (qi, hh) in enumerate(work):
        if n + ATTN_LOOKAHEAD < len(work):
            pending.append(scores(*work[n + ATTN_LOOKAHEAD]))
        finish(qi, hh, pending.pop(0))


def _attn(q_t, k3, v_t, *, tq):
    n_seq, seq_len, _ = k3.shape
    assert seq_len % tq == 0 and tq % CHUNK == 0
    return pl.pallas_call(
        functools.partial(_attn_kernel, seq_len=seq_len, tq=tq),
        grid=(n_seq, N_PAIRS),
        in_specs=[pl.BlockSpec((2 * SLOT, seq_len), lambda b, p: (p, b)),
                  pl.BlockSpec((None, seq_len, 2 * SLOT), lambda b, p: (b, 0, p)),
                  pl.BlockSpec((2 * V_HEAD, seq_len), lambda b, p: (p, b))],
        out_specs=pl.BlockSpec((2 * V_HEAD, seq_len), lambda b, p: (p, b)),
        out_shape=jax.ShapeDtypeStruct((N_HEADS * V_HEAD, n_seq * seq_len), BF16),
        compiler_params=pltpu.CompilerParams(dimension_semantics=("arbitrary", "arbitrary"),
                                             vmem_limit_bytes=VMEM_LIMIT),
        name="attn",
    )(q_t, k3, v_t)


def _kv_expand_kernel(c_ref, kr_ref, wuk_ref, wuv_ref, place_ref, k_ref, v_ref):
    c_bf = c_ref[...].astype(BF16)
    kx = _dot(c_bf, wuk_ref[...])
    kr_slot = _dot(kr_ref[...].astype(BF16), place_ref[...])
    for hd in range(N_HEADS):
        sl = slice(hd * SLOT, (hd + 1) * SLOT)
        k_ref[:, sl] = (kx[:, sl] + kr_slot).astype(BF16)
    v_ref[...] = _dot(c_bf, wuv_ref[...]).astype(BF16)


def _kv_expand(c2d, kr2d, layer, n_tok, lw, *, tm):
    assert n_tok % tm == 0
    blk0 = layer * (n_tok // tm)
    tok = lambda w: pl.BlockSpec((tm, w), lambda i: (i, 0))
    src = lambda w: pl.BlockSpec((tm, w), lambda i: (blk0 + i, 0))
    return pl.pallas_call(
        _kv_expand_kernel,
        grid=(n_tok // tm,),
        in_specs=[src(KV_LORA), src(QK_ROPE), _const_spec(lw["w_uk"].shape),
                  _const_spec(lw["w_uv"].shape), _const_spec(lw["place"].shape)],
        out_specs=(tok(N_HEADS * SLOT), tok(N_HEADS * V_HEAD)),
        out_shape=(jax.ShapeDtypeStruct((n_tok, N_HEADS * SLOT), BF16),
                   jax.ShapeDtypeStruct((n_tok, N_HEADS * V_HEAD), BF16)),
        compiler_params=pltpu.CompilerParams(dimension_semantics=("arbitrary",),
                                             vmem_limit_bytes=VMEM_LIMIT),
        name="kv_expand",
    )(c2d, kr2d, lw["w_uk"], lw["w_uv"], lw["place"])


def _attn_sample_kernel(q_ref, kp_ref, kn_ref, vp_ref, vn_ref, o_ref, *, mask_new):
    s_len = q_ref.shape[0]
    lane = lax.broadcasted_iota(jnp.int32, (s_len, 2 * V_HEAD), 1)
    outs = []
    for hh in range(2):
        sl = slice(hh * SLOT, (hh + 1) * SLOT)
        q = q_ref[:, sl]
        dn = (((1,), (1,)), ((), ()))
        s_p = lax.dot_general(q, kp_ref[:, sl], dn, preferred_element_type=F32)
        s_n = lax.dot_general(q, kn_ref[:, sl], dn, preferred_element_type=F32)
        if mask_new is not None:
            s_n = jnp.where(mask_new, s_n, NEG_INF)
        m = jnp.maximum(jnp.max(s_p, axis=-1, keepdims=True), jnp.max(s_n, axis=-1, keepdims=True))
        p_p = jnp.exp2(s_p - m)
        p_n = jnp.exp2(s_n - m)
        l = jnp.sum(p_p, axis=-1, keepdims=True) + jnp.sum(p_n, axis=-1, keepdims=True)
        acc = _dot(p_p.astype(BF16), vp_ref[...]) + _dot(p_n.astype(BF16), vn_ref[...])
        outs.append(acc / l)
    o_ref[...] = jnp.where(lane < V_HEAD, outs[0], outs[1]).astype(BF16)


def _attn_sample(q3, kp3, kn3, vp3, vn3):
    n_seq, s_len, _ = q3.shape
    past = kp3.shape[1]
    q_chunk = (past + np.arange(s_len)) // CHUNK
    assert (np.arange(past) // CHUNK).max() <= q_chunk.min()
    new_vis = q_chunk[None, :] <= q_chunk[:, None]
    assert new_vis.all(), "new-key mask would be needed"
    blk = lambda n, w: pl.BlockSpec((None, n, w), lambda b, p: (b, 0, p))
    return pl.pallas_call(
        functools.partial(_attn_sample_kernel, mask_new=None),
        grid=(n_seq, N_PAIRS),
        in_specs=[blk(s_len, 2 * SLOT), blk(past, 2 * SLOT), blk(s_len, 2 * SLOT),
                  blk(past, 2 * V_HEAD), blk(s_len, 2 * V_HEAD)],
        out_specs=blk(s_len, 2 * V_HEAD),
        out_shape=jax.ShapeDtypeStruct((n_seq, s_len, N_HEADS * V_HEAD), BF16),
        compiler_params=pltpu.CompilerParams(dimension_semantics=("arbitrary", "arbitrary"),
                                             vmem_limit_bytes=VMEM_LIMIT),
        name="attn_sample",
    )(q3, kp3, kn3, vp3, vn3)


def _post_kernel(x_ref, attn_ref, ta_ref, sgb_ref, gpost_ref, gfpre_ref, gfpost_ref,
                 wao_ref, wmg_ref, wgu_ref, wdn_ref, o_ref, *, transposed):
    if transposed:
        yb = lax.dot_general(attn_ref[...], wao_ref[...], (((0,), (0,)), ((), ())),
                             preferred_element_type=F32)
    else:
        yb = _dot(attn_ref[...], wao_ref[...])
    mixed = (ta_ref[...].astype(F32) + sgb_ref[...].astype(F32) * yb).astype(BF16)
    m = _dot(mixed, wmg_ref[...])
    x1 = x_ref[...] + _rms(m, gpost_ref[...])
    h2 = _rms(x1, gfpre_ref[...]).astype(BF16)
    g = _dot(h2, wgu_ref[:, :D_FF])
    up = _dot(h2, wgu_ref[:, D_FF:])
    act = (g * _sigmoid(g) * up).astype(BF16)
    f = _dot(act, wdn_ref[...])
    o_ref[...] = x1 + _rms(f, gfpost_ref[...])


def _post(x2d, attn2d, ta, sgb, lw, *, tm, transposed):
    n_tok = x2d.shape[0]
    tok = lambda w: pl.BlockSpec((tm, w), lambda i: (i, 0))
    vw = N_HEADS * V_HEAD
    attn_spec = pl.BlockSpec((vw, tm), lambda i: (0, i)) if transposed else tok(vw)
    return pl.pallas_call(
        functools.partial(_post_kernel, transposed=transposed),
        grid=(n_tok // tm,),
        in_specs=[tok(D_MODEL), attn_spec, tok(D_MODEL), tok(D_MODEL),
                  _const_spec((1, D_MODEL)), _const_spec((1, D_MODEL)), _const_spec((1, D_MODEL)),
                  _const_spec(lw["w_attn_out"].shape), _const_spec(lw["w_merge"].shape),
                  _const_spec(lw["w_gate_up"].shape), _const_spec(lw["w_down"].shape)],
        out_specs=tok(D_MODEL),
        out_shape=jax.ShapeDtypeStruct((n_tok, D_MODEL), F32),
        compiler_params=pltpu.CompilerParams(dimension_semantics=("arbitrary",),
                                             vmem_limit_bytes=VMEM_LIMIT),
        name="post",
    )(x2d, attn2d, ta, sgb, lw["g_post"], lw["g_fpre"], lw["g_fpost"],
      lw["w_attn_out"], lw["w_merge"], lw["w_gate_up"], lw["w_down"])


def _rot_cols(w):
    half = QK_ROPE // 2
    return jnp.concatenate([-w[..., half:], w[..., :half]], axis=-1)


def _slot_pad(rope, nope):
    zeros = jnp.zeros(rope.shape[:-1] + (SLOT - QK_ROPE - QK_NOPE,), rope.dtype)
    return jnp.concatenate([rope, zeros, nope], axis=-1)


def _layer_weights(l, w_in, norm_attn_pre, norm_attn_post, norm_q, norm_kv, w_uq, w_ukv, conv_w,
                   w_conv_out, w_attn_out, w_merge, norm_ffn_pre, norm_ffn_post, w_gate_up, w_down):
    wi = w_in[l]
    w_main = wi[:, :O_KR].astype(BF16)
    w_gate = wi[:, O_GATE:].astype(BF16)
    w_kr = wi[:, O_KR:O_GATE].astype(BF16)
    zn = jnp.zeros((D_MODEL, QK_NOPE), BF16)
    w_kr2 = jnp.concatenate([_slot_pad(w_kr, zn), _slot_pad(_rot_cols(w_kr), zn)], axis=1)

    wq = w_uq[l].astype(BF16).reshape(Q_LORA, N_HEADS, QK_NOPE + QK_ROPE)
    wq_n, wq_r = wq[..., :QK_NOPE], wq[..., QK_NOPE:]
    wq_plain = _slot_pad(wq_r, wq_n).reshape(Q_LORA, N_HEADS * SLOT)
    wq_rot = _slot_pad(_rot_cols(wq_r), jnp.zeros_like(wq_n)).reshape(Q_LORA, N_HEADS * SLOT)
    w_uqp = jnp.concatenate([wq_plain, wq_rot], axis=1)

    wkv = w_ukv[l].astype(BF16).reshape(KV_LORA, N_HEADS, QK_NOPE + V_HEAD)
    wk_n, wv = wkv[..., :QK_NOPE], wkv[..., QK_NOPE:]
    w_uk = _slot_pad(jnp.zeros((KV_LORA, N_HEADS, QK_ROPE), BF16), wk_n).reshape(KV_LORA, N_HEADS * SLOT)
    w_uv = wv.reshape(KV_LORA, N_HEADS * V_HEAD)

    place = jnp.eye(QK_ROPE, SLOT, dtype=BF16)
    row = lambda g: g[l].reshape(1, -1)
    return dict(
        w_in=w_main, w_kr=w_kr2, w_gate=w_gate, w_uq=w_uqp, w_uq_t=wq_plain.T, w_uk=w_uk, w_uv=w_uv,
        w_uv_t=w_uv.T, place=place,
        w_conv_out=w_conv_out[l].astype(BF16), w_attn_out=w_attn_out[l].astype(BF16),
        w_merge=w_merge[l].astype(BF16), w_gate_up=w_gate_up[l].astype(BF16),
        w_down=w_down[l].astype(BF16), conv_w=conv_w[l],
        g_pre=row(norm_attn_pre), g_post=row(norm_attn_post), g_q=row(norm_q), g_kv=row(norm_kv),
        g_fpre=row(norm_ffn_pre), g_fpost=row(norm_ffn_post))


def _rope_tables(n_pos):
    half = QK_ROPE // 2
    inv = ROPE_THETA ** (-jnp.arange(half, dtype=F32) / half)
    ang = jnp.arange(n_pos, dtype=jnp.int32).astype(F32)[:, None] * inv[None, :]
    cos, sin = jnp.cos(ang), jnp.sin(ang)
    ones = jnp.ones((n_pos, SLOT - QK_ROPE), F32)
    cos_t = jnp.concatenate([cos, cos, ones], axis=1)
    sin_t = jnp.concatenate([sin, sin, 0.0 * ones], axis=1)
    return cos_t, sin_t


def kernel(x_prompt, x_sample, state_conv, cache_ckv, cache_krope, w_in, norm_attn_pre,
           norm_attn_post, norm_q, norm_kv, w_uq, w_ukv, conv_w, w_conv_out, w_attn_out,
           w_merge, norm_ffn_pre, norm_ffn_post, w_gate_up, w_down):
    n_p, s_p, _ = x_prompt.shape
    n_s, s_s, _ = x_sample.shape
    depth, _, past, _ = cache_ckv.shape
    tm_proj, tm_post, tq = 256, 512, 256
    assert past % s_s == 0
    cos_t, sin_t = _rope_tables(max(s_p, past + s_s))
    cos_s = jnp.tile(cos_t[past:past + s_s], (n_s, 1))
    sin_s = jnp.tile(sin_t[past:past + s_s], (n_s, 1))
    hist_p = jnp.zeros((n_p, CONV_W - 1, D_CONV), F32)
    cache_c2d = cache_ckv.reshape(depth * n_s * past, KV_LORA)
    cache_kr2d = cache_krope.reshape(depth * n_s * past, QK_ROPE)

    xp = x_prompt.reshape(n_p * s_p, D_MODEL)
    xs = x_sample.reshape(n_s * s_s, D_MODEL)
    conv_p, ckv_p, kr_p, conv_s, ckv_s, kr_s = [], [], [], [], [], []
    for l in range(depth):
        lw = _layer_weights(l, w_in, norm_attn_pre, norm_attn_post, norm_q, norm_kv, w_uq, w_ukv,
                            conv_w, w_conv_out, w_attn_out, w_merge, norm_ffn_pre, norm_ffn_post,
                            w_gate_up, w_down)
        q_t, k, v_t, ta, sgb, c_new, kr_new, nc = _proj(xp, hist_p, cos_t, sin_t, 0, lw, n_seq=n_p,
                                                        seq_len=s_p, tm=tm_proj, transposed=True)
        attn_t = _attn(q_t, k.reshape(n_p, s_p, -1), v_t, tq=tq)
        xp = _post(xp, attn_t, ta, sgb, lw, tm=tm_post, transposed=True)
        conv_p.append(nc)
        ckv_p.append(c_new.reshape(n_p, s_p, KV_LORA))
        kr_p.append(kr_new.reshape(n_p, s_p, QK_ROPE))
        q, k, v, ta, sgb, c_new, kr_new, nc = _proj(xs, state_conv[l], cos_s, sin_s, 0, lw, n_seq=n_s,
                                                    seq_len=s_s, tm=n_s * s_s, transposed=False)
        k_past, v_past = _kv_expand(cache_c2d, cache_kr2d, l, n_s * past, lw, tm=512)
        attn = _attn_sample(q.reshape(n_s, s_s, -1), k_past.reshape(n_s, past, -1),
                            k.reshape(n_s, s_s, -1), v_past.reshape(n_s, past, -1),
                            v.reshape(n_s, s_s, -1))
        xs = _post(xs, attn.reshape(n_s * s_s, -1), ta, sgb, lw, tm=n_s * s_s, transposed=False)
        conv_s.append(nc)
        ckv_s.append(c_new.reshape(n_s, s_s, KV_LORA))
        kr_s.append(kr_new.reshape(n_s, s_s, QK_ROPE))

    return (xp.reshape(n_p, s_p, D_MODEL), xs.reshape(n_s, s_s, D_MODEL),
            jnp.stack(conv_p), jnp.stack(ckv_p), jnp.stack(kr_p),
            jnp.stack(conv_s), jnp.stack(ckv_s), jnp.stack(kr_s))
```

```python
import functools
import math

import jax
import jax.numpy as jnp
import numpy as np
from jax import lax
from jax.experimental import pallas as pl
from jax.experimental.pallas import tpu as pltpu

D_MODEL = 1024
CHUNK = 64
D_CONV = 1024
CONV_W = 3
N_HEADS = 16
QK_NOPE = 64
QK_ROPE = 32
V_HEAD = 64
Q_LORA = 384
KV_LORA = 256
D_FF = 2816
ROPE_THETA = 10000.0
EPS = 1e-6
ATTN_SCALE = (QK_NOPE + QK_ROPE) ** -0.5
NEG_INF = -1e30
Q_SCALE = ATTN_SCALE * math.log2(math.e)

LANES = 128
SLOT = LANES
N_PAIRS = N_HEADS // 2
O_B, O_C, O_X = 0, D_CONV, 2 * D_CONV
O_QL = 3 * D_CONV
O_KV = O_QL + Q_LORA
O_KR = O_KV + KV_LORA
O_GATE = O_KR + QK_ROPE
assert O_KR % LANES == 0
VMEM_LIMIT = 56 * 1024 * 1024
ATTN_LOOKAHEAD = 8

F32 = jnp.float32
BF16 = jnp.bfloat16


def _rms(x, g):
    return x * lax.rsqrt(jnp.mean(x * x, axis=-1, keepdims=True) + EPS) * g


def _sigmoid(x):
    return 1.0 / (1.0 + jnp.exp(-x))


def _dot(a, b):
    return jnp.dot(a, b, preferred_element_type=F32)


def _const_spec(shape):
    nd = len(shape)
    return pl.BlockSpec(shape, lambda *_: (0,) * nd, pipeline_mode=pl.Buffered(1))


def _proj_kernel(x_ref, hist_ref, cos_ref, sin_ref, gpre_ref, gq_ref, gkv_ref, cw_ref,
                 win_ref, wkr_ref, wg_ref, wuq_ref, wuk_ref, wuv_ref, wco_ref,
                 q_ref, k_ref, v_ref, ta_ref, sgb_ref, c_ref, kr_ref, nc_ref,
                 ubuf, *, tm, tiles_per_seq, seqs_per_tile, transposed):
    seg = tm // seqs_per_tile
    if seqs_per_tile == 1:
        first = (pl.program_id(0) % tiles_per_seq) == 0

        @pl.when(first)
        def _():
            ubuf[6:8, :] = hist_ref[0]

        @pl.when(jnp.logical_not(first))
        def _():
            ubuf[0:8, :] = ubuf[tm:tm + 8, :]

    x = x_ref[...]
    h = _rms(x, gpre_ref[...]).astype(BF16)
    cx = _dot(h, win_ref[:, O_C:O_C + 2 * D_CONV])
    u = cx[:, :D_CONV] * cx[:, D_CONV:]
    cw = cw_ref[...]
    conv_parts = []
    for sq in range(seqs_per_tile):
        u_seg = u[sq * seg:(sq + 1) * seg]
        if seqs_per_tile > 1:
            ubuf[6:8, :] = hist_ref[sq]
        ubuf[8:8 + seg, :] = u_seg
        conv_parts.append(cw[0:1, :] * ubuf[6:6 + seg, :] + cw[1:2, :] * ubuf[7:7 + seg, :]
                          + cw[2:3, :] * u_seg)
        nc_ref[sq] = ubuf[seg + 6:seg + 8, :]
    conv = conv_parts[0] if seqs_per_tile == 1 else jnp.concatenate(conv_parts, axis=0)
    bg = _dot(h, win_ref[:, O_B:O_B + D_CONV])
    ya = _dot((bg * conv).astype(BF16), wco_ref[...])
    ga = _dot(h, wg_ref[:, :D_MODEL])
    ta_ref[...] = (_sigmoid(ga) * ya).astype(BF16)
    gb = _dot(h, wg_ref[:, D_MODEL:])
    sgb_ref[...] = _sigmoid(gb).astype(BF16)

    cos = cos_ref[...]
    sin = sin_ref[...]
    small = _dot(h, win_ref[:, O_QL:O_KR])
    qn = _rms(small[:, :Q_LORA], gq_ref[...])
    c_new = _rms(small[:, Q_LORA:], gkv_ref[...])
    c_ref[...] = c_new
    kr2 = _dot(h, wkr_ref[...])
    kr_slot = kr2[:, :SLOT] * cos + kr2[:, SLOT:] * sin
    kr_ref[...] = kr_slot[:, :QK_ROPE]

    c_bf = c_new.astype(BF16)
    kx = _dot(c_bf, wuk_ref[...])
    for hd in range(N_HEADS):
        sl = slice(hd * SLOT, (hd + 1) * SLOT)
        k_ref[:, sl] = (kx[:, sl] + kr_slot).astype(BF16)

    qw = N_HEADS * SLOT
    if transposed:
        half = QK_ROPE // 2
        cos_r, sin_r = cos.T[:QK_ROPE], sin.T[:QK_ROPE]
        qq = _dot(wuq_ref[...], qn.T.astype(BF16))
        for hd in range(N_HEADS):
            a = qq[hd * SLOT:(hd + 1) * SLOT, :]
            ar = a[:QK_ROPE]
            rot = jnp.concatenate([-ar[half:], ar[:half]], axis=0)
            q_ref[hd * SLOT:hd * SLOT + QK_ROPE, :] = ((ar * cos_r + rot * sin_r) * Q_SCALE).astype(BF16)
            q_ref[hd * SLOT + QK_ROPE:(hd + 1) * SLOT, :] = (a[QK_ROPE:] * Q_SCALE).astype(BF16)
        v_ref[...] = _dot(wuv_ref[...], c_new.T.astype(BF16)).astype(BF16)
    else:
        qq = _dot(qn.astype(BF16), wuq_ref[...])
        for hd in range(N_HEADS):
            sl = slice(hd * SLOT, (hd + 1) * SLOT)
            a = qq[:, sl]
            b = qq[:, qw + hd * SLOT:qw + (hd + 1) * SLOT]
            q_ref[:, sl] = ((a * cos + b * sin) * Q_SCALE).astype(BF16)
        v_ref[...] = _dot(c_bf, wuv_ref[...]).astype(BF16)


def _proj(x2d, hist, cos_t, sin_t, pos_tile0, lw, *, n_seq, seq_len, tm, transposed):
    n_tok = n_seq * seq_len
    tiles_per_seq = max(seq_len // tm, 1)
    seqs_per_tile = max(tm // seq_len, 1)
    assert tiles_per_seq * tm == seq_len * seqs_per_tile and seq_len >= CONV_W - 1
    assert n_tok % tm == 0
    grid = (n_tok // tm,)
    tok = lambda w: pl.BlockSpec((tm, w), lambda i: (i, 0))
    tok_t = lambda w: pl.BlockSpec((w, tm), lambda i: (0, i))
    pos = pl.BlockSpec((tm, SLOT), lambda i: (pos_tile0 + i % tiles_per_seq, 0))
    seq3 = pl.BlockSpec((seqs_per_tile, CONV_W - 1, D_CONV), lambda i: (i // tiles_per_seq, 0, 0))
    qw, vw = N_HEADS * SLOT, N_HEADS * V_HEAD
    if transposed:
        q_shape, v_shape, q_spec, v_spec = (qw, n_tok), (vw, n_tok), tok_t(qw), tok_t(vw)
        w_uq, w_uv = lw["w_uq_t"], lw["w_uv_t"]
    else:
        q_shape, v_shape, q_spec, v_spec = (n_tok, qw), (n_tok, vw), tok(qw), tok(vw)
        w_uq, w_uv = lw["w_uq"], lw["w_uv"]
    out_shape = (
        jax.ShapeDtypeStruct(q_shape, BF16),
        jax.ShapeDtypeStruct((n_tok, qw), BF16),
        jax.ShapeDtypeStruct(v_shape, BF16),
        jax.ShapeDtypeStruct((n_tok, D_MODEL), BF16),
        jax.ShapeDtypeStruct((n_tok, D_MODEL), BF16),
        jax.ShapeDtypeStruct((n_tok, KV_LORA), F32),
        jax.ShapeDtypeStruct((n_tok, QK_ROPE), F32),
        jax.ShapeDtypeStruct((n_seq, CONV_W - 1, D_CONV), F32),
    )
    return pl.pallas_call(
        functools.partial(_proj_kernel, tm=tm, tiles_per_seq=tiles_per_seq,
                          seqs_per_tile=seqs_per_tile, transposed=transposed),
        grid=grid,
        in_specs=[tok(D_MODEL), seq3, pos, pos,
                  _const_spec((1, D_MODEL)), _const_spec((1, Q_LORA)), _const_spec((1, KV_LORA)),
                  _const_spec((CONV_W, D_CONV)),
                  _const_spec(lw["w_in"].shape), _const_spec(lw["w_kr"].shape),
                  _const_spec(lw["w_gate"].shape), _const_spec(w_uq.shape), _const_spec(lw["w_uk"].shape),
                  _const_spec(w_uv.shape), _const_spec(lw["w_conv_out"].shape)],
        out_specs=(q_spec, tok(qw), v_spec,
                   tok(D_MODEL), tok(D_MODEL), tok(KV_LORA), tok(QK_ROPE), seq3),
        out_shape=out_shape,
        scratch_shapes=[pltpu.VMEM((tm // seqs_per_tile + 8, D_CONV), F32)],
        compiler_params=pltpu.CompilerParams(dimension_semantics=("arbitrary",),
                                             vmem_limit_bytes=VMEM_LIMIT),
        name="proj",
    )(x2d, hist, cos_t, sin_t, lw["g_pre"], lw["g_q"], lw["g_kv"], lw["conv_w"],
      lw["w_in"], lw["w_kr"], lw["w_gate"], w_uq, lw["w_uk"], w_uv, lw["w_conv_out"])


def _attn_kernel(qt_ref, k_ref, vt_ref, ot_ref, *, seq_len, tq):
    n_q = seq_len // tq
    key = lax.broadcasted_iota(jnp.int32, (tq, tq), 0)
    qry = lax.broadcasted_iota(jnp.int32, (tq, tq), 1)
    diag_mask = (key // CHUNK) <= (qry // CHUNK)
    def scores(qi, hh):
        q_t = qt_ref[hh * SLOT:(hh + 1) * SLOT, qi * tq:(qi + 1) * tq]
        hs = slice(hh * SLOT, (hh + 1) * SLOT)
        out = [_dot(k_ref[0:qi * tq, hs], q_t)] if qi > 0 else []
        out.append(jnp.where(diag_mask, _dot(k_ref[qi * tq:(qi + 1) * tq, hs], q_t), NEG_INF))
        return out

    def finish(qi, hh, s_parts):
        m = functools.reduce(jnp.maximum, [jnp.max(s, axis=0, keepdims=True) for s in s_parts])
        p_parts = [jnp.exp2(s - m) for s in s_parts]
        l = functools.reduce(jnp.add, [jnp.sum(p, axis=0, keepdims=True) for p in p_parts])
        vs = slice(hh * V_HEAD, (hh + 1) * V_HEAD)
        bounds = ([(0, qi * tq)] if qi > 0 else []) + [(qi * tq, (qi + 1) * tq)]
        acc = functools.reduce(jnp.add, [_dot(vt_ref[vs, a:b], p.astype(BF16))
                                         for (a, b), p in zip(bounds, p_parts)])
        ot_ref[vs, qi * tq:(qi + 1) * tq] = (acc * (1.0 / l)).astype(BF16)

    work = [(qi, hh) for qi in range(n_q) for hh in range(2)]
    pending = [scores(*w) for w in work[:ATTN_LOOKAHEAD]]
    for n, (qi, hh) in enumerate(work):
        if n + ATTN_LOOKAHEAD < len(work):
            pending.append(scores(*work[n + ATTN_LOOKAHEAD]))
        finish(qi, hh, pending.pop(0))


def _attn(q_t, k3, v_t, *, tq):
    n_seq, seq_len, _ = k3.shape
    assert seq_len % tq == 0 and tq % CHUNK == 0
    return pl.pallas_call(
        functools.partial(_attn_kernel, seq_len=seq_len, tq=tq),
        grid=(n_seq, N_PAIRS),
        in_specs=[pl.BlockSpec((2 * SLOT, seq_len), lambda b, p: (p, b)),
                  pl.BlockSpec((None, seq_len, 2 * SLOT), lambda b, p: (b, 0, p)),
                  pl.BlockSpec((2 * V_HEAD, seq_len), lambda b, p: (p, b))],
        out_specs=pl.BlockSpec((2 * V_HEAD, seq_len), lambda b, p: (p, b)),
        out_shape=jax.ShapeDtypeStruct((N_HEADS * V_HEAD, n_seq * seq_len), BF16),
        compiler_params=pltpu.CompilerParams(dimension_semantics=("arbitrary", "arbitrary"),
                                             vmem_limit_bytes=VMEM_LIMIT),
        name="attn",
    )(q_t, k3, v_t)


def _attn_sample_kernel(q_ref, c_ref, kr_ref, kn_ref, vn_ref, wuk_ref, wuv_ref, place_ref, o_ref):
    s_len = q_ref.shape[0]
    lane = lax.broadcasted_iota(jnp.int32, (s_len, 2 * V_HEAD), 1)
    dn = (((1,), (1,)), ((), ()))
    c_bf = c_ref[...].astype(BF16)
    kr_slot = _dot(kr_ref[...].astype(BF16), place_ref[...])
    for pr in range(N_PAIRS):
        kx = _dot(c_bf, wuk_ref[:, pr * 2 * SLOT:(pr + 1) * 2 * SLOT])
        vs = slice(pr * 2 * V_HEAD, (pr + 1) * 2 * V_HEAD)
        v_p = _dot(c_bf, wuv_ref[:, vs]).astype(BF16)
        v_n = vn_ref[:, vs]
        outs = []
        for hh in range(2):
            k_p = (kx[:, hh * SLOT:(hh + 1) * SLOT] + kr_slot).astype(BF16)
            sl = slice((2 * pr + hh) * SLOT, (2 * pr + hh + 1) * SLOT)
            q = q_ref[:, sl]
            s_p = lax.dot_general(q, k_p, dn, preferred_element_type=F32)
            s_n = lax.dot_general(q, kn_ref[:, sl], dn, preferred_element_type=F32)
            m = jnp.maximum(jnp.max(s_p, axis=-1, keepdims=True), jnp.max(s_n, axis=-1, keepdims=True))
            p_p = jnp.exp2(s_p - m)
            p_n = jnp.exp2(s_n - m)
            l = jnp.sum(p_p, axis=-1, keepdims=True) + jnp.sum(p_n, axis=-1, keepdims=True)
            acc = _dot(p_p.astype(BF16), v_p) + _dot(p_n.astype(BF16), v_n)
            outs.append(acc / l)
        o_ref[:, vs] = jnp.where(lane < V_HEAD, outs[0], outs[1]).astype(BF16)


def _attn_sample(q3, c2d, kr2d, layer, past, kn3, vn3, lw):
    n_seq, s_len, _ = q3.shape
    q_chunk = (past + np.arange(s_len)) // CHUNK
    assert (np.arange(past) // CHUNK).max() <= q_chunk.min()
    assert (q_chunk[None, :] <= q_chunk[:, None]).all(), "new-key mask would be needed"
    seq = lambda w: pl.BlockSpec((None, s_len, w), lambda b: (b, 0, 0))
    src = lambda w: pl.BlockSpec((past, w), lambda b: (layer * n_seq + b, 0))
    return pl.pallas_call(
        _attn_sample_kernel,
        grid=(n_seq,),
        in_specs=[seq(N_HEADS * SLOT), src(KV_LORA), src(QK_ROPE), seq(N_HEADS * SLOT),
                  seq(N_HEADS * V_HEAD), _const_spec(lw["w_uk"].shape), _const_spec(lw["w_uv"].shape),
                  _const_spec(lw["place"].shape)],
        out_specs=seq(N_HEADS * V_HEAD),
        out_shape=jax.ShapeDtypeStruct((n_seq, s_len, N_HEADS * V_HEAD), BF16),
        compiler_params=pltpu.CompilerParams(dimension_semantics=("arbitrary",),
                                             vmem_limit_bytes=VMEM_LIMIT),
        name="attn_sample",
    )(q3, c2d, kr2d, kn3, vn3, lw["w_uk"], lw["w_uv"], lw["place"])


def _post_kernel(x_ref, attn_ref, ta_ref, sgb_ref, gpost_ref, gfpre_ref, gfpost_ref,
                 wao_ref, wmg_ref, wgu_ref, wdn_ref, o_ref, *, transposed):
    if transposed:
        yb = lax.dot_general(attn_ref[...], wao_ref[...], (((0,), (0,)), ((), ())),
                             preferred_element_type=F32)
    else:
        yb = _dot(attn_ref[...], wao_ref[...])
    mixed = (ta_ref[...].astype(F32) + sgb_ref[...].astype(F32) * yb).astype(BF16)
    m = _dot(mixed, wmg_ref[...])
    x1 = x_ref[...] + _rms(m, gpost_ref[...])
    h2 = _rms(x1, gfpre_ref[...]).astype(BF16)
    g = _dot(h2, wgu_ref[:, :D_FF])
    up = _dot(h2, wgu_ref[:, D_FF:])
    act = (g * _sigmoid(g) * up).astype(BF16)
    f = _dot(act, wdn_ref[...])
    o_ref[...] = x1 + _rms(f, gfpost_ref[...])


def _post(x2d, attn2d, ta, sgb, lw, *, tm, transposed):
    n_tok = x2d.shape[0]
    tok = lambda w: pl.BlockSpec((tm, w), lambda i: (i, 0))
    vw = N_HEADS * V_HEAD
    attn_spec = pl.BlockSpec((vw, tm), lambda i: (0, i)) if transposed else tok(vw)
    return pl.pallas_call(
        functools.partial(_post_kernel, transposed=transposed),
        grid=(n_tok // tm,),
        in_specs=[tok(D_MODEL), attn_spec, tok(D_MODEL), tok(D_MODEL),
                  _const_spec((1, D_MODEL)), _const_spec((1, D_MODEL)), _const_spec((1, D_MODEL)),
                  _const_spec(lw["w_attn_out"].shape), _const_spec(lw["w_merge"].shape),
                  _const_spec(lw["w_gate_up"].shape), _const_spec(lw["w_down"].shape)],
        out_specs=tok(D_MODEL),
        out_shape=jax.ShapeDtypeStruct((n_tok, D_MODEL), F32),
        compiler_params=pltpu.CompilerParams(dimension_semantics=("arbitrary",),
                                             vmem_limit_bytes=VMEM_LIMIT),
        name="post",
    )(x2d, attn2d, ta, sgb, lw["g_post"], lw["g_fpre"], lw["g_fpost"],
      lw["w_attn_out"], lw["w_merge"], lw["w_gate_up"], lw["w_down"])


def _rot_cols(w):
    half = QK_ROPE // 2
    return jnp.concatenate([-w[..., half:], w[..., :half]], axis=-1)


def _slot_pad(rope, nope):
    zeros = jnp.zeros(rope.shape[:-1] + (SLOT - QK_ROPE - QK_NOPE,), rope.dtype)
    return jnp.concatenate([rope, zeros, nope], axis=-1)


def _layer_weights(l, w_in, norm_attn_pre, norm_attn_post, norm_q, norm_kv, w_uq, w_ukv, conv_w,
                   w_conv_out, w_attn_out, w_merge, norm_ffn_pre, norm_ffn_post, w_gate_up, w_down):
    wi = w_in[l]
    w_main = wi[:, :O_KR].astype(BF16)
    w_gate = wi[:, O_GATE:].astype(BF16)
    w_kr = wi[:, O_KR:O_GATE].astype(BF16)
    zn = jnp.zeros((D_MODEL, QK_NOPE), BF16)
    w_kr2 = jnp.concatenate([_slot_pad(w_kr, zn), _slot_pad(_rot_cols(w_kr), zn)], axis=1)

    wq = w_uq[l].astype(BF16).reshape(Q_LORA, N_HEADS, QK_NOPE + QK_ROPE)
    wq_n, wq_r = wq[..., :QK_NOPE], wq[..., QK_NOPE:]
    wq_plain = _slot_pad(wq_r, wq_n).reshape(Q_LORA, N_HEADS * SLOT)
    wq_rot = _slot_pad(_rot_cols(wq_r), jnp.zeros_like(wq_n)).reshape(Q_LORA, N_HEADS * SLOT)
    w_uqp = jnp.concatenate([wq_plain, wq_rot], axis=1)

    wkv = w_ukv[l].astype(BF16).reshape(KV_LORA, N_HEADS, QK_NOPE + V_HEAD)
    wk_n, wv = wkv[..., :QK_NOPE], wkv[..., QK_NOPE:]
    w_uk = _slot_pad(jnp.zeros((KV_LORA, N_HEADS, QK_ROPE), BF16), wk_n).reshape(KV_LORA, N_HEADS * SLOT)
    w_uv = wv.reshape(KV_LORA, N_HEADS * V_HEAD)

    place = jnp.eye(QK_ROPE, SLOT, dtype=BF16)
    row = lambda g: g[l].reshape(1, -1)
    return dict(
        w_in=w_main, w_kr=w_kr2, w_gate=w_gate, w_uq=w_uqp, w_uq_t=wq_plain.T, w_uk=w_uk, w_uv=w_uv,
        w_uv_t=w_uv.T, place=place,
        w_conv_out=w_conv_out[l].astype(BF16), w_attn_out=w_attn_out[l].astype(BF16),
        w_merge=w_merge[l].astype(BF16), w_gate_up=w_gate_up[l].astype(BF16),
        w_down=w_down[l].astype(BF16), conv_w=conv_w[l],
        g_pre=row(norm_attn_pre), g_post=row(norm_attn_post), g_q=row(norm_q), g_kv=row(norm_kv),
        g_fpre=row(norm_ffn_pre), g_fpost=row(norm_ffn_post))


def _rope_tables(n_pos):
    half = QK_ROPE // 2
    inv = ROPE_THETA ** (-jnp.arange(half, dtype=F32) / half)
    ang = jnp.arange(n_pos, dtype=jnp.int32).astype(F32)[:, None] * inv[None, :]
    cos, sin = jnp.cos(ang), jnp.sin(ang)
    ones = jnp.ones((n_pos, SLOT - QK_ROPE), F32)
    cos_t = jnp.concatenate([cos, cos, ones], axis=1)
    sin_t = jnp.concatenate([sin, sin, 0.0 * ones], axis=1)
    return cos_t, sin_t


def kernel(x_prompt, x_sample, state_conv, cache_ckv, cache_krope, w_in, norm_attn_pre,
           norm_attn_post, norm_q, norm_kv, w_uq, w_ukv, conv_w, w_conv_out, w_attn_out,
           w_merge, norm_ffn_pre, norm_ffn_post, w_gate_up, w_down):
    n_p, s_p, _ = x_prompt.shape
    n_s, s_s, _ = x_sample.shape
    depth, _, past, _ = cache_ckv.shape
    tm_proj, tm_post, tq = 256, 512, 256
    assert past % s_s == 0
    cos_t, sin_t = _rope_tables(max(s_p, past + s_s))
    cos_s = jnp.tile(cos_t[past:past + s_s], (n_s, 1))
    sin_s = jnp.tile(sin_t[past:past + s_s], (n_s, 1))
    hist_p = jnp.zeros((n_p, CONV_W - 1, D_CONV), F32)
    cache_c2d = cache_ckv.reshape(depth * n_s * past, KV_LORA)
    cache_kr2d = cache_krope.reshape(depth * n_s * past, QK_ROPE)

    xp = x_prompt.reshape(n_p * s_p, D_MODEL)
    xs = x_sample.reshape(n_s * s_s, D_MODEL)
    conv_p, ckv_p, kr_p, conv_s, ckv_s, kr_s = [], [], [], [], [], []
    for l in range(depth):
        lw = _layer_weights(l, w_in, norm_attn_pre, norm_attn_post, norm_q, norm_kv, w_uq, w_ukv,
                            conv_w, w_conv_out, w_attn_out, w_merge, norm_ffn_pre, norm_ffn_post,
                            w_gate_up, w_down)
        q_t, k, v_t, ta, sgb, c_new, kr_new, nc = _proj(xp, hist_p, cos_t, sin_t, 0, lw, n_seq=n_p,
                                                        seq_len=s_p, tm=tm_proj, transposed=True)
        attn_t = _attn(q_t, k.reshape(n_p, s_p, -1), v_t, tq=tq)
        xp = _post(xp, attn_t, ta, sgb, lw, tm=tm_post, transposed=True)
        conv_p.append(nc)
        ckv_p.append(c_new.reshape(n_p, s_p, KV_LORA))
        kr_p.append(kr_new.reshape(n_p, s_p, QK_ROPE))
        q, k, v, ta, sgb, c_new, kr_new, nc = _proj(xs, state_conv[l], cos_s, sin_s, 0, lw, n_seq=n_s,
                                                    seq_len=s_s, tm=n_s * s_s, transposed=False)
        attn = _attn_sample(q.reshape(n_s, s_s, -1), cache_c2d, cache_kr2d, l, past,
                            k.reshape(n_s, s_s, -1), v.reshape(n_s, s_s, -1), lw)
        xs = _post(xs, attn.reshape(n_s * s_s, -1), ta, sgb, lw, tm=n_s * s_s, transposed=False)
        conv_s.append(nc)
        ckv_s.append(c_new.reshape(n_s, s_s, KV_LORA))
        kr_s.append(kr_new.reshape(n_s, s_s, QK_ROPE))

    return (xp.reshape(n_p, s_p, D_MODEL), xs.reshape(n_s, s_s, D_MODEL),
            jnp.stack(conv_p), jnp.stack(ckv_p), jnp.stack(kr_p),
            jnp.stack(conv_s), jnp.stack(ckv_s), jnp.stack(kr_s))
```
